```python
import jax, jax.numpy as jnp
from jax import lax
import numpy as np

D_MODEL = 1024
BATCH = 4
SEQ = 8192
DEPTH = 4
DEC_BATCH = 16
DEC_SEQ = 32
PAST_LEN = 1024

CHUNK = 64
SSM_WIDTH = 256
SSM_GROUP = 16
SSM_GROUPS = SSM_WIDTH // SSM_GROUP
SSM_STATE = 64
SGU_WIDTH = 256
SGU_CHUNK = 128
SGU_GROUPS = 4
SGU_GROUP_DIM = SGU_WIDTH // SGU_GROUPS
SB_HEADS = 8
SB_HEAD_DIM = 64
SB_WIDTH = SB_HEADS * SB_HEAD_DIM
SB_BLOCK = 128
N_BRANCH = 3
FFN_HIDDEN = ((8 * D_MODEL + 3 * 256 - 1) // (3 * 256)) * 256
IN_WIDTH = SSM_WIDTH + 2 * SGU_WIDTH + 3 * SB_WIDTH + N_BRANCH * D_MODEL
IN_SPLITS = (SSM_WIDTH,
             SSM_WIDTH + SGU_WIDTH,
             SSM_WIDTH + 2 * SGU_WIDTH,
             SSM_WIDTH + 2 * SGU_WIDTH + SB_WIDTH,
             SSM_WIDTH + 2 * SGU_WIDTH + 2 * SB_WIDTH,
             SSM_WIDTH + 2 * SGU_WIDTH + 3 * SB_WIDTH)
EPS = 1e-6

kernel_name = "hybrid_stream_s5_gmlp_stickbreak_step"


def rmsnorm(x, g):
    xf = x.astype(jnp.float32)
    y = xf * lax.rsqrt(jnp.mean(xf * xf, axis=-1, keepdims=True) + EPS)
    return (y * g.astype(jnp.float32)).astype(x.dtype)


def layernorm(x, g):
    xf = x.astype(jnp.float32)
    mu = jnp.mean(xf, axis=-1, keepdims=True)
    xc = xf - mu
    y = xc * lax.rsqrt(jnp.mean(xc * xc, axis=-1, keepdims=True) + EPS)
    return (y * g.astype(jnp.float32)).astype(x.dtype)


def s5_discretize(a_re, a_im, b_re, b_im, log_dt):
    f32 = jnp.float32
    a_re, a_im, b_re, b_im = a_re.astype(f32), a_im.astype(f32), b_re.astype(f32), b_im.astype(f32)
    dt = jnp.exp(log_dt.astype(f32))[:, None]
    mag = jnp.exp(a_re * dt)
    ang = a_im * dt
    ab_re = mag * jnp.cos(ang)
    ab_im = mag * jnp.sin(ang)
    num_re = ab_re - 1.0
    num_im = ab_im
    den = a_re * a_re + a_im * a_im
    f_re = (num_re * a_re + num_im * a_im) / den
    f_im = (num_im * a_re - num_re * a_im) / den
    bb_re = f_re[..., None] * b_re - f_im[..., None] * b_im
    bb_im = f_re[..., None] * b_im + f_im[..., None] * b_re
    return ab_re, ab_im, bb_re, bb_im


def _complex_affine_combine(e1, e2):
    a1r, a1i, b1r, b1i = e1
    a2r, a2i, b2r, b2i = e2
    return (a2r * a1r - a2i * a1i,
            a2r * a1i + a2i * a1r,
            a2r * b1r - a2i * b1i + b2r,
            a2r * b1i + a2i * b1r + b2i)


def s5_mixer(u, s0_re, s0_im, a_re, a_im, b_re, b_im, c_re, c_im, d, log_dt, w_glu):
    f32 = jnp.float32
    bsz, L, _ = u.shape
    uf = u.astype(f32)
    ug = uf.reshape(bsz, L, SSM_GROUPS, SSM_GROUP)
    ab_re, ab_im, bb_re, bb_im = s5_discretize(a_re, a_im, b_re, b_im, log_dt)
    bu_re = jnp.einsum("blgc,gpc->blgp", ug, bb_re)
    bu_im = jnp.einsum("blgc,gpc->blgp", ug, bb_im)
    s0_re = s0_re.astype(f32)
    s0_im = s0_im.astype(f32)
    bu_re = bu_re.at[:, 0].add(ab_re * s0_re - ab_im * s0_im)
    bu_im = bu_im.at[:, 0].add(ab_re * s0_im + ab_im * s0_re)
    shape = bu_re.shape
    elems = (jnp.broadcast_to(ab_re, shape), jnp.broadcast_to(ab_im, shape), bu_re, bu_im)
    _, _, st_re, st_im = lax.associative_scan(_complex_affine_combine, elems, axis=1)
    y = (jnp.einsum("blgp,gcp->blgc", st_re, c_re.astype(f32))
         - jnp.einsum("blgp,gcp->blgc", st_im, c_im.astype(f32)))
    y = y.reshape(bsz, L, SSM_WIDTH) + d.astype(f32) * uf
    z = jax.nn.gelu(y) @ w_glu.astype(f32)
    out = z[..., :SSM_WIDTH] * jax.nn.sigmoid(z[..., SSM_WIDTH:])
    return out.astype(u.dtype), st_re[:, -1], st_im[:, -1]


def sgu_mixer(u, v, ln_g, w_s, b_s):
    bsz, L, _ = v.shape
    vn = layernorm(v, ln_g)
    n = min(L, SGU_CHUNK)
    nchunks = L // n
    mask = jnp.tril(jnp.ones((n, n), dtype=bool))
    w = jnp.where(mask[None], w_s[:, :n, :n], 0.0)
    vc = vn.reshape(bsz, nchunks, n, SGU_GROUPS, SGU_GROUP_DIM)
    mixed = jnp.einsum("gts,bnsgc->bntgc", w, vc) + b_s[:, :n].T[None, None, :, :, None]
    out = u * mixed.reshape(bsz, L, SGU_WIDTH)
    return out.astype(u.dtype), vn


def stick_breaking(q, k, v, q_pos, k_pos):
    f32 = jnp.float32
    z = jnp.einsum("bhtd,bhsd->bhts", q.astype(f32), k.astype(f32)) * (SB_HEAD_DIM ** -0.5)
    causal = k_pos[None, :] < q_pos[:, None]
    log_beta = jax.nn.log_sigmoid(z)
    log_1m = jnp.where(causal, jax.nn.log_sigmoid(-z), 0.0)
    between = lax.cumsum(log_1m, axis=3, reverse=True) - log_1m
    wts = jnp.where(causal, jnp.exp(log_beta + between), 0.0)
    return jnp.einsum("bhts,bhsd->bhtd", wts, v.astype(f32))


def stick_breaking_blocks(q, k, v):
    bsz, H, L, Dh = q.shape
    nb = L // SB_BLOCK
    qb = q.reshape(bsz, H, nb, SB_BLOCK, Dh).transpose(2, 0, 1, 3, 4)
    k_pos = jnp.arange(L)

    def one_block(args):
        qi, bi = args
        q_pos = bi * SB_BLOCK + jnp.arange(SB_BLOCK)
        return stick_breaking(qi, k, v, q_pos, k_pos)

    out = lax.map(one_block, (qb, jnp.arange(nb)))
    return out.transpose(1, 2, 0, 3, 4).reshape(bsz, H, L, Dh)


def trunk_layer(h, ssm_re0, ssm_im0, k_past, v_past,
                norm_mix, w_in, a_re, a_im, b_re, b_im, c_re, c_im, d, log_dt, w_glu,
                sgu_norm, sgu_w, sgu_b, w_branch_a, w_branch_b, w_branch_c, w_out,
                norm_ffn, w_gate_up, w_down):
    bsz, L, _ = h.shape
    xn = rmsnorm(h, norm_mix)
    proj = xn @ w_in
    u_a, u_b, v_b, q, k, v, gates = jnp.split(proj, IN_SPLITS, axis=-1)
    y_a, st_re, st_im = s5_mixer(u_a, ssm_re0, ssm_im0, a_re, a_im, b_re, b_im,
                                 c_re, c_im, d, log_dt, w_glu)
    y_b, vn_b = sgu_mixer(u_b, v_b, sgu_norm, sgu_w, sgu_b)
    heads = lambda t: t.reshape(bsz, L, SB_HEADS, SB_HEAD_DIM).transpose(0, 2, 1, 3)
    qh, kh, vh = heads(q), heads(k), heads(v)
    if k_past is None:
        o = stick_breaking_blocks(qh, kh, vh)
    else:
        past = k_past.shape[2]
        k_all = jnp.concatenate([k_past.astype(kh.dtype), kh], axis=2)
        v_all = jnp.concatenate([v_past.astype(vh.dtype), vh], axis=2)
        o = stick_breaking(qh, k_all, v_all, past + jnp.arange(L), jnp.arange(past + L))
    y_c = o.transpose(0, 2, 1, 3).reshape(bsz, L, SB_WIDTH).astype(h.dtype)
    g = jax.nn.sigmoid(gates.astype(jnp.float32)).reshape(bsz, L, N_BRANCH, D_MODEL)
    merged = (g[:, :, 0] * (y_a @ w_branch_a) + g[:, :, 1] * (y_b @ w_branch_b)
              + g[:, :, 2] * (y_c @ w_branch_c))
    h = h + (merged.astype(h.dtype) @ w_out).astype(h.dtype)
    xn2 = rmsnorm(h, norm_ffn)
    gu = xn2 @ w_gate_up
    gate, up = gu[..., :FFN_HIDDEN], gu[..., FFN_HIDDEN:]
    h = h + ((jax.nn.silu(gate) * up) @ w_down).astype(h.dtype)
    return h, st_re, st_im, kh, vh, vn_b


def setup_inputs(seed: int = 0) -> dict:
    key = jax.random.key(seed)
    ks = jax.random.split(key, 32)
    f32 = jnp.float32
    nrm = lambda k, shape, s: jax.random.normal(k, shape, f32) * s
    a_im_base = jnp.pi * jnp.arange(SSM_STATE, dtype=f32)
    log_dt = jnp.log(1e-3) + jax.random.uniform(ks[10], (DEPTH, SSM_GROUPS), f32) * (jnp.log(1e-1) - jnp.log(1e-3))
    return {
        "x_prompt": nrm(ks[0], (BATCH, SEQ, D_MODEL), 1.0),
        "x_sample": nrm(ks[1], (DEC_BATCH, DEC_SEQ, D_MODEL), 1.0),
        "state_ssm_re": nrm(ks[2], (DEPTH, DEC_BATCH, SSM_GROUPS, SSM_STATE), 0.5),
        "state_ssm_im": nrm(ks[3], (DEPTH, DEC_BATCH, SSM_GROUPS, SSM_STATE), 0.5),
        "cache_sb_k": nrm(ks[4], (DEPTH, DEC_BATCH, SB_HEADS, PAST_LEN, SB_HEAD_DIM), 1.0),
        "cache_sb_v": nrm(ks[5], (DEPTH, DEC_BATCH, SB_HEADS, PAST_LEN, SB_HEAD_DIM), 1.0),
        "norm_mix": 1.0 + nrm(ks[6], (DEPTH, D_MODEL), 0.02),
        "w_in": nrm(ks[7], (DEPTH, D_MODEL, IN_WIDTH), D_MODEL ** -0.5),
        "ssm_a_re": -0.5 + nrm(ks[8], (DEPTH, SSM_GROUPS, SSM_STATE), 0.01),
        "ssm_a_im": a_im_base + nrm(ks[9], (DEPTH, SSM_GROUPS, SSM_STATE), 0.01),
        "ssm_b_re": nrm(ks[11], (DEPTH, SSM_GROUPS, SSM_STATE, SSM_GROUP), (2 * SSM_GROUP) ** -0.5),
        "ssm_b_im": nrm(ks[12], (DEPTH, SSM_GROUPS, SSM_STATE, SSM_GROUP), (2 * SSM_GROUP) ** -0.5),
        "ssm_c_re": nrm(ks[13], (DEPTH, SSM_GROUPS, SSM_GROUP, SSM_STATE), (2 * SSM_STATE) ** -0.5),
        "ssm_c_im": nrm(ks[14], (DEPTH, SSM_GROUPS, SSM_GROUP, SSM_STATE), (2 * SSM_STATE) ** -0.5),
        "ssm_d": nrm(ks[15], (DEPTH, SSM_WIDTH), 1.0),
        "ssm_log_dt": log_dt,
        "ssm_w_glu": nrm(ks[16], (DEPTH, SSM_WIDTH, 2 * SSM_WIDTH), SSM_WIDTH ** -0.5),
        "sgu_norm": 1.0 + nrm(ks[17], (DEPTH, SGU_WIDTH), 0.02),
        "sgu_w": nrm(ks[18], (DEPTH, SGU_GROUPS, SGU_CHUNK, SGU_CHUNK), SGU_CHUNK ** -0.5),
        "sgu_b": 1.0 + nrm(ks[19], (DEPTH, SGU_GROUPS, SGU_CHUNK), 0.02),
        "w_branch_a": nrm(ks[20], (DEPTH, SSM_WIDTH, D_MODEL), SSM_WIDTH ** -0.5),
        "w_branch_b": nrm(ks[21], (DEPTH, SGU_WIDTH, D_MODEL), SGU_WIDTH ** -0.5),
        "w_branch_c": nrm(ks[22], (DEPTH, SB_WIDTH, D_MODEL), SB_WIDTH ** -0.5),
        "w_out": nrm(ks[23], (DEPTH, D_MODEL, D_MODEL), D_MODEL ** -0.5),
        "norm_ffn": 1.0 + nrm(ks[24], (DEPTH, D_MODEL), 0.02),
        "w_gate_up": nrm(ks[25], (DEPTH, D_MODEL, 2 * FFN_HIDDEN), D_MODEL ** -0.5),
        "w_down": nrm(ks[26], (DEPTH, FFN_HIDDEN, D_MODEL), FFN_HIDDEN ** -0.5),
        "norm_final": 1.0 + nrm(ks[27], (D_MODEL,), 0.02),
    }


def reference(x_prompt, x_sample, state_ssm_re, state_ssm_im, cache_sb_k, cache_sb_v,
              norm_mix, w_in, ssm_a_re, ssm_a_im, ssm_b_re, ssm_b_im, ssm_c_re, ssm_c_im,
              ssm_d, ssm_log_dt, ssm_w_glu, sgu_norm, sgu_w, sgu_b,
              w_branch_a, w_branch_b, w_branch_c, w_out, norm_ffn, w_gate_up, w_down,
              norm_final):
    layer_weights = (norm_mix, w_in, ssm_a_re, ssm_a_im, ssm_b_re, ssm_b_im, ssm_c_re, ssm_c_im,
                     ssm_d, ssm_log_dt, ssm_w_glu, sgu_norm, sgu_w, sgu_b,
                     w_branch_a, w_branch_b, w_branch_c, w_out, norm_ffn, w_gate_up, w_down)
    zero_state = jnp.zeros((x_prompt.shape[0], SSM_GROUPS, SSM_STATE), jnp.float32)
    hp, hs = x_prompt, x_sample
    p_re, p_im, p_k, p_v = [], [], [], []
    s_re, s_im, s_k, s_v, s_vb = [], [], [], [], []
    for i in range(DEPTH):
        lw = [w[i] for w in layer_weights]
        hp, a_re, a_im, kh, vh, _ = trunk_layer(hp, zero_state, zero_state, None, None, *lw)
        p_re.append(a_re); p_im.append(a_im); p_k.append(kh); p_v.append(vh)
        hs, b_re, b_im, kd, vd, vb = trunk_layer(hs, state_ssm_re[i], state_ssm_im[i],
                                                 cache_sb_k[i], cache_sb_v[i], *lw)
        s_re.append(b_re); s_im.append(b_im); s_k.append(kd); s_v.append(vd); s_vb.append(vb)
    y_prompt = rmsnorm(hp, norm_final)
    y_sample = rmsnorm(hs, norm_final)
    return (y_prompt, y_sample,
            jnp.stack(p_re), jnp.stack(p_im), jnp.stack(p_k), jnp.stack(p_v),
            jnp.stack(s_re), jnp.stack(s_im), jnp.stack(s_k), jnp.stack(s_v), jnp.stack(s_vb))
```

```python
import functools
import math

import jax
import jax.numpy as jnp
from jax import lax
from jax.experimental import pallas as pl
from jax.experimental.pallas import tpu as pltpu

F32 = jnp.float32
BF16 = jnp.bfloat16

D_MODEL = 1024
SSM_WIDTH = 256
SSM_GROUP = 16
SSM_GROUPS = 16
SSM_STATE = 64
N_STATE = SSM_GROUPS * SSM_STATE
SGU_WIDTH = 256
SGU_CHUNK = 128
SGU_GROUPS = 4
SGU_GROUP_DIM = SGU_WIDTH // SGU_GROUPS
SB_HEADS = 8
SB_HEAD_DIM = 64
SB_WIDTH = SB_HEADS * SB_HEAD_DIM
FFN_HIDDEN = 2816
MIX_WIDTH = SSM_WIDTH + 2 * SGU_WIDTH
QKV_OFFSET = MIX_WIDTH
GATE_OFFSET = MIX_WIDTH + 3 * SB_WIDTH
EPS = 1e-6

SUBLANES = 8
VMEM_LIMIT = 56 * 1024 * 1024
EXP_UNDERFLOW = -100.0
ATTN_BLOCK = 256
FFN_CHUNK = 256
TOKEN_TILE = 512


def _rms(x, g):
    return x * lax.rsqrt(jnp.mean(x * x, axis=-1, keepdims=True) + EPS) * g


def _dot(a, b):
    return jnp.dot(a, b, preferred_element_type=F32)


def _sigmoid(x):
    return 1.0 / (1.0 + jnp.exp(-x))


def _split_bf16(x):
    hi = x.astype(BF16)
    lo = (x - hi.astype(F32)).astype(BF16)
    return hi, lo


def _params(semantics):
    return pltpu.CompilerParams(dimension_semantics=semantics, vmem_limit_bytes=VMEM_LIMIT)


def _resident(shape):
    zeros = (0,) * len(shape)
    return pl.BlockSpec(shape, lambda *_: zeros, pipeline_mode=pl.Buffered(1))


def _inproj_kernel(h_ref, g_ref, w_ref, uav_ref, q_ref, k_ref, v_ref, *, nb, tl):
    x = h_ref[...].reshape(nb * tl, D_MODEL)
    xn = _rms(x, g_ref[...]).astype(BF16)
    uav_ref[...] = _dot(xn, w_ref[:, 0:MIX_WIDTH]).reshape(nb, tl, MIX_WIDTH)
    for idx, ref in enumerate((q_ref, k_ref, v_ref)):
        c0 = QKV_OFFSET + idx * SB_WIDTH
        seg = _dot(xn, w_ref[:, c0:c0 + SB_WIDTH])
        for bb in range(nb):
            for hh in range(SB_HEADS):
                ref[bb, hh] = seg[bb * tl:(bb + 1) * tl, hh * SB_HEAD_DIM:(hh + 1) * SB_HEAD_DIM]


def _inproj(h, g, w_mix, nb, tl):
    bsz, L, _ = h.shape
    grid = (bsz // nb, L // tl)
    head_shape = jax.ShapeDtypeStruct((bsz, SB_HEADS, L, SB_HEAD_DIM), F32)
    head_spec = pl.BlockSpec((nb, SB_HEADS, tl, SB_HEAD_DIM), lambda b, i: (b, 0, i, 0))
    return pl.pallas_call(
        functools.partial(_inproj_kernel, nb=nb, tl=tl),
        grid=grid,
        in_specs=[pl.BlockSpec((nb, tl, D_MODEL), lambda b, i: (b, i, 0)),
                  _resident((1, D_MODEL)),
                  _resident(w_mix.shape)],
        out_specs=[pl.BlockSpec((nb, tl, MIX_WIDTH), lambda b, i: (b, i, 0)),
                   head_spec, head_spec, head_spec],
        out_shape=[jax.ShapeDtypeStruct((bsz, L, MIX_WIDTH), F32),
                   head_shape, head_shape, head_shape],
        compiler_params=_params(("parallel", "parallel")),
        name="inproj",
    )(h, g, w_mix)


def _gelu_tanh(y):
    c = math.sqrt(2.0 / math.pi)
    return 0.5 * y * (1.0 + jnp.tanh(c * (y + 0.044715 * (y * y * y))))


def _mixab_kernel(uav_ref, s0_ref, bhi_ref, blo_ref, tab_ref, cmat_ref, d_ref, wglu_ref,
                  lng_ref, wsgu_ref, bsgu_ref,
                  ya_ref, yb_ref, vn_ref, sfin_ref,
                  x_scr, carry_scr, *, tb, n):
    i = pl.program_id(1)

    @pl.when(i == 0)
    def _():
        carry_scr[...] = s0_ref[0]

    u = uav_ref[0, :, 0:SSM_WIDTH]
    u_hi, u_lo = _split_bf16(u)
    bhi = bhi_ref[...]
    x_scr[...] = _dot(u_hi, bhi) + _dot(u_lo, bhi) + _dot(u_hi, blo_ref[...])

    half = N_STATE // 2

    def tile_scan(j, carry):
        r0 = pl.multiple_of(j * SUBLANES, SUBLANES)
        new_carry = []
        for hf in range(2):
            re_cols = pl.ds(hf * half, half)
            im_cols = pl.ds(N_STATE + hf * half, half)
            xr = x_scr[pl.ds(r0, SUBLANES), re_cols]
            xi = x_scr[pl.ds(r0, SUBLANES), im_cols]
            for k in range(3):
                sr = pltpu.roll(xr, 1 << k, 0)
                si = pltpu.roll(xi, 1 << k, 0)
                ar = tab_ref[2 * k, :, re_cols]
                ai = tab_ref[2 * k + 1, :, re_cols]
                xr, xi = xr + (ar * sr - ai * si), xi + (ar * si + ai * sr)
            cr, ci = carry[2 * hf], carry[2 * hf + 1]
            pr = tab_ref[6, :, re_cols]
            pi_ = tab_ref[7, :, re_cols]
            xr, xi = xr + (pr * cr - pi_ * ci), xi + (pr * ci + pi_ * cr)
            x_scr[pl.ds(r0, SUBLANES), re_cols] = xr
            x_scr[pl.ds(r0, SUBLANES), im_cols] = xi
            new_carry += [xr[SUBLANES - 1:SUBLANES], xi[SUBLANES - 1:SUBLANES]]
        return tuple(new_carry)

    carry0 = (carry_scr[:, 0:half], carry_scr[:, N_STATE:N_STATE + half],
              carry_scr[:, half:N_STATE], carry_scr[:, N_STATE + half:2 * N_STATE])
    c = lax.fori_loop(0, tb // SUBLANES, tile_scan, carry0)
    carry_scr[:, 0:half] = c[0]
    carry_scr[:, N_STATE:N_STATE + half] = c[1]
    carry_scr[:, half:N_STATE] = c[2]
    carry_scr[:, N_STATE + half:2 * N_STATE] = c[3]
    sfin_ref[0] = carry_scr[...]

    y = _dot(x_scr[...].astype(BF16), cmat_ref[...]) + d_ref[...] * u
    z = _dot(_gelu_tanh(y).astype(BF16), wglu_ref[...])
    ya_ref[0] = (z[:, 0:SSM_WIDTH] * _sigmoid(z[:, SSM_WIDTH:])).astype(ya_ref.dtype)

    ub = uav_ref[0, :, SSM_WIDTH:SSM_WIDTH + SGU_WIDTH]
    vb = uav_ref[0, :, SSM_WIDTH + SGU_WIDTH:MIX_WIDTH]
    xc = vb - jnp.mean(vb, axis=-1, keepdims=True)
    vn = xc * lax.rsqrt(jnp.mean(xc * xc, axis=-1, keepdims=True) + EPS) * lng_ref[...]
    vn_ref[0] = vn
    vnb = vn.astype(BF16)
    row = lax.broadcasted_iota(jnp.int32, (n, n), 0)
    col = lax.broadcasted_iota(jnp.int32, (n, n), 1)
    group = lax.broadcasted_iota(jnp.int32, (n, SGU_WIDTH), 1) // SGU_GROUP_DIM
    w_tril = [jnp.where(row >= col, wsgu_ref[g], 0.0).astype(BF16) for g in range(SGU_GROUPS)]
    for ch in range(tb // n):
        rows = slice(ch * n, (ch + 1) * n)
        mixed = bsgu_ref[...]
        for g in range(SGU_GROUPS):
            mixed = mixed + jnp.where(group == g, _dot(w_tril[g], vnb[rows]), 0.0)
        yb_ref[0, rows] = (ub[rows] * mixed).astype(yb_ref.dtype)


def _mixab(uav, s0, lp, tb, n):
    bsz, L, _ = uav.shape
    grid = (bsz, L // tb)
    tok = lambda width, dtype: (pl.BlockSpec((1, tb, width), lambda b, i: (b, i, 0)),
                                jax.ShapeDtypeStruct((bsz, L, width), dtype))
    ya_spec, ya_shape = tok(SSM_WIDTH, BF16)
    yb_spec, yb_shape = tok(SGU_WIDTH, BF16)
    vn_spec, vn_shape = tok(SGU_WIDTH, F32)
    state_spec = pl.BlockSpec((1, 1, 2 * N_STATE), lambda b, i: (b, 0, 0))
    return pl.pallas_call(
        functools.partial(_mixab_kernel, tb=tb, n=n),
        grid=grid,
        in_specs=[pl.BlockSpec((1, tb, MIX_WIDTH), lambda b, i: (b, i, 0)),
                  state_spec,
                  _resident(lp["b_hi"].shape), _resident(lp["b_lo"].shape),
                  _resident(lp["tab"].shape), _resident(lp["cmat"].shape),
                  _resident(lp["d"].shape), _resident(lp["w_glu"].shape),
                  _resident(lp["sgu_norm"].shape), _resident((SGU_GROUPS, n, n)),
                  _resident((n, SGU_WIDTH))],
        out_specs=[ya_spec, yb_spec, vn_spec, state_spec],
        out_shape=[ya_shape, yb_shape, vn_shape,
                   jax.ShapeDtypeStruct((bsz, 1, 2 * N_STATE), F32)],
        scratch_shapes=[pltpu.VMEM((tb, 2 * N_STATE), F32), pltpu.VMEM((1, 2 * N_STATE), F32)],
        compiler_params=_params(("parallel", "arbitrary")),
        name="mixab",
    )(uav, s0, lp["b_hi"], lp["b_lo"], lp["tab"], lp["cmat"], lp["d"], lp["w_glu"],
      lp["sgu_norm"], lp["sgu_w"][:, :n, :n], lp["sgu_bias"][:n])


def _attn_kernel(q_ref, kd_ref, vd_ref, kh_ref, vh_ref, o_ref, *, tq, tk, hist0, hist_per_q):
    i = pl.program_id(2)
    q = (q_ref[0, 0] * (SB_HEAD_DIM ** -0.5)).astype(BF16)

    def suffix_ones(size):
        r = lax.broadcasted_iota(jnp.int32, (size, size), 0)
        c = lax.broadcasted_iota(jnp.int32, (size, size), 1)
        return jnp.where(r >= c, 1.0, 0.0).astype(BF16)

    def sweep(kb, vb, ones, visible, surv, acc):
        z = lax.dot_general(q, kb, (((1,), (1,)), ((), ())), preferred_element_type=F32)
        log_1m = -(jnp.maximum(z, 0.0) + jnp.log(1.0 + jnp.exp(-jnp.abs(z))))
        if visible is not None:
            log_1m = jnp.where(visible, log_1m, 0.0)
        hi, lo = _split_bf16(log_1m)
        suffix = _dot(hi, ones) + _dot(lo, ones)
        wts = jnp.exp(z + suffix + surv)
        if visible is not None:
            wts = jnp.where(visible, wts, 0.0)
        acc = acc + _dot(wts.astype(BF16), vb)
        return surv + suffix[:, 0:1], acc

    r = lax.broadcasted_iota(jnp.int32, (tq, tq), 0)
    c = lax.broadcasted_iota(jnp.int32, (tq, tq), 1)
    surv, acc = sweep(kd_ref[0, 0].astype(BF16), vd_ref[0, 0].astype(BF16), suffix_ones(tq),
                      c < r, jnp.zeros((tq, 1), F32), jnp.zeros((tq, SB_HEAD_DIM), F32))

    ones_k = suffix_ones(tk)

    def more(carry):
        j, surv, _ = carry
        return jnp.logical_and(j >= 0, jnp.max(surv) > EXP_UNDERFLOW)

    def step(carry):
        j, surv, acc = carry
        rows = pl.ds(pl.multiple_of(j * tk, tk), tk)
        surv, acc = sweep(kh_ref[0, 0, rows, :].astype(BF16), vh_ref[0, 0, rows, :].astype(BF16),
                          ones_k, None, surv, acc)
        return j - 1, surv, acc

    _, _, acc = lax.while_loop(more, step, (hist0 + i * hist_per_q - 1, surv, acc))
    o_ref[0, 0] = acc


def _attention(q, k_new, v_new, k_hist, v_hist, tq, tk, hist0, hist_per_q):
    bsz, H, L, Dh = q.shape
    Lh = k_hist.shape[2]
    blk = pl.BlockSpec((1, 1, tq, Dh), lambda b, h, i: (b, h, i, 0))
    hist = pl.BlockSpec((1, 1, Lh, Dh), lambda b, h, i: (b, h, 0, 0))
    return pl.pallas_call(
        functools.partial(_attn_kernel, tq=tq, tk=tk, hist0=hist0, hist_per_q=hist_per_q),
        grid=(bsz, H, L // tq),
        in_specs=[blk, blk, blk, hist, hist],
        out_specs=blk,
        out_shape=jax.ShapeDtypeStruct((bsz, H, L, Dh), F32),
        compiler_params=_params(("parallel", "parallel", "arbitrary")),
        name="stickbreak",
    )(q, k_new, v_new, k_hist, v_hist)


def _merge_kernel(h_ref, g_ref, wg_ref, ya_ref, yb_ref, o_ref, wa_ref, wb_ref, wc_ref, wout_ref,
                  hout_ref, *, nb, tl):
    m = nb * tl
    x = h_ref[...].reshape(m, D_MODEL)
    xn = _rms(x, g_ref[...]).astype(BF16)

    def gate(idx):
        return _sigmoid(_dot(xn, wg_ref[:, idx * D_MODEL:(idx + 1) * D_MODEL]))

    merged = gate(0) * _dot(ya_ref[...].reshape(m, SSM_WIDTH), wa_ref[...])
    merged = merged + gate(1) * _dot(yb_ref[...].reshape(m, SGU_WIDTH), wb_ref[...])
    pc = jnp.zeros((m, D_MODEL), F32)
    for hh in range(SB_HEADS):
        oh = jnp.concatenate([o_ref[bb, hh] for bb in range(nb)], axis=0) if nb > 1 else o_ref[0, hh]
        pc = pc + _dot(oh.astype(BF16), wc_ref[hh * SB_HEAD_DIM:(hh + 1) * SB_HEAD_DIM, :])
    merged = merged + gate(2) * pc
    hout_ref[...] = (x + _dot(merged.astype(BF16), wout_ref[...])).reshape(nb, tl, D_MODEL)


def _merge(h, ya, yb, o, lp, nb, tl):
    bsz, L, _ = h.shape
    tok = lambda width: pl.BlockSpec((nb, tl, width), lambda b, i: (b, i, 0))
    return pl.pallas_call(
        functools.partial(_merge_kernel, nb=nb, tl=tl),
        grid=(bsz // nb, L // tl),
        in_specs=[tok(D_MODEL), _resident((1, D_MODEL)), _resident(lp["w_gates"].shape),
                  tok(SSM_WIDTH), tok(SGU_WIDTH),
                  pl.BlockSpec((nb, SB_HEADS, tl, SB_HEAD_DIM), lambda b, i: (b, 0, i, 0)),
                  _resident(lp["w_branch_a"].shape), _resident(lp["w_branch_b"].shape),
                  _resident(lp["w_branch_c"].shape), _resident(lp["w_out"].shape)],
        out_specs=tok(D_MODEL),
        out_shape=jax.ShapeDtypeStruct(h.shape, F32),
        compiler_params=_params(("parallel", "parallel")),
        name="merge",
    )(h, lp["norm_mix"], lp["w_gates"], ya, yb, o, lp["w_branch_a"], lp["w_branch_b"],
      lp["w_branch_c"], lp["w_out"])


def _ffn_kernel(h_ref, g_ref, wgu_ref, wd_ref, gfin_ref, out_ref, act_scr, *, final):
    x = h_ref[...]
    xn = _rms(x, g_ref[...]).astype(BF16)
    for c0 in range(0, FFN_HIDDEN, FFN_CHUNK):
        gate = _dot(xn, wgu_ref[:, c0:c0 + FFN_CHUNK])
        up = _dot(xn, wgu_ref[:, FFN_HIDDEN + c0:FFN_HIDDEN + c0 + FFN_CHUNK])
        act_scr[:, c0:c0 + FFN_CHUNK] = (gate * _sigmoid(gate) * up).astype(BF16)
    y = x + _dot(act_scr[...], wd_ref[...])
    if final:
        y = _rms(y, gfin_ref[...])
    out_ref[...] = y


def _ffn(h2d, lp, g_final, final, tm):
    T = h2d.shape[0]
    tok = pl.BlockSpec((tm, D_MODEL), lambda i: (i, 0))
    return pl.pallas_call(
        functools.partial(_ffn_kernel, final=final),
        grid=(T // tm,),
        in_specs=[tok, _resident((1, D_MODEL)), _resident(lp["w_gate_up"].shape),
                  _resident(lp["w_down"].shape), _resident((1, D_MODEL))],
        out_specs=tok,
        out_shape=jax.ShapeDtypeStruct(h2d.shape, F32),
        scratch_shapes=[pltpu.VMEM((tm, FFN_HIDDEN), BF16)],
        compiler_params=_params(("parallel",)),
        name="ffn",
    )(h2d, lp["norm_ffn"], lp["w_gate_up"], lp["w_down"], g_final)


def _cmul(a, b):
    return a[0] * b[0] - a[1] * b[1], a[0] * b[1] + a[1] * b[0]


def _s5_tables(a_re, a_im, b_re, b_im, c_re, c_im, log_dt):
    dt = jnp.exp(log_dt)[:, None]
    mag = jnp.exp(a_re * dt)
    ab = (mag * jnp.cos(a_im * dt), mag * jnp.sin(a_im * dt))
    num_re, num_im = ab[0] - 1.0, ab[1]
    den = a_re * a_re + a_im * a_im
    f_re = (num_re * a_re + num_im * a_im) / den
    f_im = (num_im * a_re - num_re * a_im) / den
    bb_re = f_re[..., None] * b_re - f_im[..., None] * b_im
    bb_im = f_re[..., None] * b_im + f_im[..., None] * b_re
    eye = jnp.eye(SSM_GROUPS, dtype=F32)
    bd_in = lambda m: jnp.einsum("gpc,gh->gchp", m, eye).reshape(SSM_WIDTH, N_STATE)
    bmat = jnp.concatenate([bd_in(bb_re), bd_in(bb_im)], axis=1)
    b_hi, b_lo = _split_bf16(bmat)
    bd_out = lambda m: jnp.einsum("gcp,gh->gphc", m, eye).reshape(N_STATE, SSM_WIDTH)
    cmat = jnp.concatenate([bd_out(c_re), -bd_out(c_im)], axis=0).astype(BF16)
    a1 = (ab[0].reshape(1, N_STATE), ab[1].reshape(1, N_STATE))
    pows = [a1]
    for _ in range(SUBLANES - 1):
        pows.append(_cmul(pows[-1], a1))
    row = jnp.arange(SUBLANES)[:, None]
    tab = []
    for k in range(3):
        p = pows[(1 << k) - 1]
        keep = row >= (1 << k)
        tab += [jnp.where(keep, p[0], 0.0), jnp.where(keep, p[1], 0.0)]
    tab += [jnp.concatenate([p[0] for p in pows], axis=0),
            jnp.concatenate([p[1] for p in pows], axis=0)]
    return b_hi, b_lo, cmat, jnp.stack(tab)


def _layer_params(i, norm_mix, w_in, ssm_a_re, ssm_a_im, ssm_b_re, ssm_b_im, ssm_c_re, ssm_c_im,
                  ssm_d, ssm_log_dt, ssm_w_glu, sgu_norm, sgu_w, sgu_b,
                  w_branch_a, w_branch_b, w_branch_c, w_out, norm_ffn, w_gate_up, w_down):
    b_hi, b_lo, cmat, tab = _s5_tables(ssm_a_re[i], ssm_a_im[i], ssm_b_re[i], ssm_b_im[i],
                                       ssm_c_re[i], ssm_c_im[i], ssm_log_dt[i])
    w_in_b = w_in[i].astype(BF16)
    return {
        "norm_mix": norm_mix[i].reshape(1, D_MODEL),
        "w_mix": w_in_b[:, :GATE_OFFSET],
        "w_gates": w_in_b[:, GATE_OFFSET:],
        "b_hi": b_hi, "b_lo": b_lo, "cmat": cmat, "tab": tab,
        "d": ssm_d[i].reshape(1, SSM_WIDTH),
        "w_glu": ssm_w_glu[i].astype(BF16),
        "sgu_norm": sgu_norm[i].reshape(1, SGU_WIDTH),
        "sgu_w": sgu_w[i],
        "sgu_bias": jnp.repeat(sgu_b[i].T, SGU_GROUP_DIM, axis=1),
        "w_branch_a": w_branch_a[i].astype(BF16),
        "w_branch_b": w_branch_b[i].astype(BF16),
        "w_branch_c": w_branch_c[i].astype(BF16),
        "w_out": w_out[i].astype(BF16),
        "norm_ffn": norm_ffn[i].reshape(1, D_MODEL),
        "w_gate_up": w_gate_up[i].astype(BF16),
        "w_down": w_down[i].astype(BF16),
    }


def _layer(h, s0, k_past, v_past, lp, g_final, final):
    bsz, L, _ = h.shape
    if k_past is None:
        nb, tl = 1, min(L, TOKEN_TILE)
        tb, n = tl, min(L, SGU_CHUNK)
    else:
        nb, tl = bsz, L
        tb, n = L, L
    uav, q, k, v = _inproj(h, lp["norm_mix"], lp["w_mix"], nb, tl)
    ya, yb, vn, sfin = _mixab(uav, s0, lp, tb, n)
    if k_past is None:
        tq = min(L, ATTN_BLOCK)
        o = _attention(q, k, v, k, v, tq, tq, 0, 1)
    else:
        past = k_past.shape[2]
        tk = min(past, ATTN_BLOCK)
        o = _attention(q, k, v, k_past, v_past, L, tk, past // tk, 0)
    h1 = _merge(h, ya, yb, o, lp, nb, tl)
    T = bsz * L
    h2 = _ffn(h1.reshape(T, D_MODEL), lp, g_final, final, min(T, TOKEN_TILE))
    return h2.reshape(bsz, L, D_MODEL), sfin, k, v, vn


def kernel(x_prompt, x_sample, state_ssm_re, state_ssm_im, cache_sb_k, cache_sb_v, norm_mix, w_in, ssm_a_re, ssm_a_im, ssm_b_re, ssm_b_im, ssm_c_re, ssm_c_im, ssm_d, ssm_log_dt, ssm_w_glu, sgu_norm, sgu_w, sgu_b, w_branch_a, w_branch_b, w_branch_c, w_out, norm_ffn, w_gate_up, w_down, norm_final):
    weights = (norm_mix, w_in, ssm_a_re, ssm_a_im, ssm_b_re, ssm_b_im, ssm_c_re, ssm_c_im,
               ssm_d, ssm_log_dt, ssm_w_glu, sgu_norm, sgu_w, sgu_b,
               w_branch_a, w_branch_b, w_branch_c, w_out, norm_ffn, w_gate_up, w_down)
    depth = w_in.shape[0]
    bp, bs = x_prompt.shape[0], x_sample.shape[0]
    g_final = norm_final.reshape(1, D_MODEL)
    zero_state = jnp.zeros((bp, 1, 2 * N_STATE), F32)
    hp, hs = x_prompt, x_sample
    outs = [[] for _ in range(9)]
    for i in range(depth):
        lp = _layer_params(i, *weights)
        final = i == depth - 1
        hp, sfin_p, kp, vp, _ = _layer(hp, zero_state, None, None, lp, g_final, final)
        s0 = jnp.concatenate([state_ssm_re[i].reshape(bs, 1, N_STATE),
                              state_ssm_im[i].reshape(bs, 1, N_STATE)], axis=-1)
        hs, sfin_s, ks, vs, vn_s = _layer(hs, s0, cache_sb_k[i], cache_sb_v[i], lp, g_final, final)
        state = lambda s, part: s[:, 0, part * N_STATE:(part + 1) * N_STATE].reshape(
            -1, SSM_GROUPS, SSM_STATE)
        for lst, val in zip(outs, (state(sfin_p, 0), state(sfin_p, 1), kp, vp,
                                   state(sfin_s, 0), state(sfin_s, 1), ks, vs, vn_s)):
            lst.append(val)
    return (hp, hs) + tuple(jnp.stack(lst) for lst in outs)
```

```python
import functools
import math

import jax
import jax.numpy as jnp
from jax import lax
from jax.experimental import pallas as pl
from jax.experimental.pallas import tpu as pltpu

F32 = jnp.float32
BF16 = jnp.bfloat16

D_MODEL = 1024
SSM_WIDTH = 256
SSM_GROUP = 16
SSM_GROUPS = 16
SSM_STATE = 64
N_STATE = SSM_GROUPS * SSM_STATE
SGU_WIDTH = 256
SGU_CHUNK = 128
SGU_GROUPS = 4
SGU_GROUP_DIM = SGU_WIDTH // SGU_GROUPS
SB_HEADS = 8
SB_HEAD_DIM = 64
SB_WIDTH = SB_HEADS * SB_HEAD_DIM
FFN_HIDDEN = 2816
MIX_WIDTH = SSM_WIDTH + 2 * SGU_WIDTH
QKV_OFFSET = MIX_WIDTH
GATE_OFFSET = MIX_WIDTH + 3 * SB_WIDTH
EPS = 1e-6

SUBLANES = 8
VMEM_LIMIT = 56 * 1024 * 1024
EXP_UNDERFLOW = -100.0
ATTN_BLOCK = 256
FFN_CHUNK = 256
TOKEN_TILE = 512


def _rms(x, g):
    return x * lax.rsqrt(jnp.mean(x * x, axis=-1, keepdims=True) + EPS) * g


def _dot(a, b):
    return jnp.dot(a, b, preferred_element_type=F32)


def _sigmoid(x):
    return 1.0 / (1.0 + jnp.exp(-x))


def _split_bf16(x):
    hi = x.astype(BF16)
    lo = (x - hi.astype(F32)).astype(BF16)
    return hi, lo


def _params(semantics):
    return pltpu.CompilerParams(dimension_semantics=semantics, vmem_limit_bytes=VMEM_LIMIT)


def _resident(shape):
    zeros = (0,) * len(shape)
    return pl.BlockSpec(shape, lambda *_: zeros, pipeline_mode=pl.Buffered(1))


def _inproj_kernel(h_ref, g_ref, w_ref, k_in, v_in, uav_ref, q_ref, k_ref, v_ref, *, nb, tl):
    del k_in, v_in
    x = h_ref[...].reshape(nb * tl, D_MODEL)
    xn = _rms(x, g_ref[...]).astype(BF16)
    uav_ref[...] = _dot(xn, w_ref[:, 0:MIX_WIDTH]).reshape(nb, tl, MIX_WIDTH)
    for idx, ref in enumerate((q_ref, k_ref.at[0], v_ref.at[0])):
        c0 = QKV_OFFSET + idx * SB_WIDTH
        seg = _dot(xn, w_ref[:, c0:c0 + SB_WIDTH])
        for bb in range(nb):
            for hh in range(SB_HEADS):
                ref[bb, hh] = seg[bb * tl:(bb + 1) * tl, hh * SB_HEAD_DIM:(hh + 1) * SB_HEAD_DIM]


def _inproj(h, g, w_mix, k_stack, v_stack, layer, depth, nb, tl):
    bsz, L, _ = h.shape
    grid = (bsz // nb, L // tl)
    head_shape = jax.ShapeDtypeStruct((bsz, SB_HEADS, L, SB_HEAD_DIM), F32)
    head_spec = pl.BlockSpec((nb, SB_HEADS, tl, SB_HEAD_DIM), lambda b, i: (b, 0, i, 0))
    stack_shape = jax.ShapeDtypeStruct((depth, bsz, SB_HEADS, L, SB_HEAD_DIM), F32)
    stack_spec = pl.BlockSpec((1, nb, SB_HEADS, tl, SB_HEAD_DIM), lambda b, i: (layer, b, 0, i, 0))
    if k_stack is None:
        k_stack = jnp.zeros(stack_shape.shape, F32)
        v_stack = jnp.zeros(stack_shape.shape, F32)
    any_spec = pl.BlockSpec(memory_space=pl.ANY)
    return pl.pallas_call(
        functools.partial(_inproj_kernel, nb=nb, tl=tl),
        grid=grid,
        in_specs=[pl.BlockSpec((nb, tl, D_MODEL), lambda b, i: (b, i, 0)),
                  _resident((1, D_MODEL)),
                  _resident(w_mix.shape), any_spec, any_spec],
        out_specs=[pl.BlockSpec((nb, tl, MIX_WIDTH), lambda b, i: (b, i, 0)),
                   head_spec, stack_spec, stack_spec],
        out_shape=[jax.ShapeDtypeStruct((bsz, L, MIX_WIDTH), F32),
                   head_shape, stack_shape, stack_shape],
        input_output_aliases={3: 2, 4: 3},
        compiler_params=_params(("parallel", "parallel")),
        name="inproj",
    )(h, g, w_mix, k_stack, v_stack)


def _gelu_tanh(y):
    c = math.sqrt(2.0 / math.pi)
    return 0.5 * y * (1.0 + jnp.tanh(c * (y + 0.044715 * (y * y * y))))


def _mixab_kernel(uav_ref, s0_ref, bmat_ref, tab_ref, cmat_ref, d_ref, wglu_ref,
                  lng_ref, wsgu_ref, bsgu_ref,
                  ya_ref, yb_ref, vn_ref, sfin_ref,
                  x_scr, carry_scr, *, tb, n):
    i = pl.program_id(1)

    @pl.when(i == 0)
    def _():
        carry_scr[...] = s0_ref[0]

    u = uav_ref[0, :, 0:SSM_WIDTH]
    x_scr[...] = _dot(u.astype(BF16), bmat_ref[...])

    half = N_STATE // 2

    def tile_scan(j, carry):
        r0 = pl.multiple_of(j * SUBLANES, SUBLANES)
        new_carry = []
        for hf in range(2):
            re_cols = pl.ds(hf * half, half)
            im_cols = pl.ds(N_STATE + hf * half, half)
            xr = x_scr[pl.ds(r0, SUBLANES), re_cols]
            xi = x_scr[pl.ds(r0, SUBLANES), im_cols]
            for k in range(3):
                sr = pltpu.roll(xr, 1 << k, 0)
                si = pltpu.roll(xi, 1 << k, 0)
                ar = tab_ref[2 * k, :, re_cols]
                ai = tab_ref[2 * k + 1, :, re_cols]
                xr, xi = xr + (ar * sr - ai * si), xi + (ar * si + ai * sr)
            cr, ci = carry[2 * hf], carry[2 * hf + 1]
            pr = tab_ref[6, :, re_cols]
            pi_ = tab_ref[7, :, re_cols]
            xr, xi = xr + (pr * cr - pi_ * ci), xi + (pr * ci + pi_ * cr)
            x_scr[pl.ds(r0, SUBLANES), re_cols] = xr
            x_scr[pl.ds(r0, SUBLANES), im_cols] = xi
            new_carry += [xr[SUBLANES - 1:SUBLANES], xi[SUBLANES - 1:SUBLANES]]
        return tuple(new_carry)

    carry0 = (carry_scr[:, 0:half], carry_scr[:, N_STATE:N_STATE + half],
              carry_scr[:, half:N_STATE], carry_scr[:, N_STATE + half:2 * N_STATE])
    c = lax.fori_loop(0, tb // SUBLANES, tile_scan, carry0)
    carry_scr[:, 0:half] = c[0]
    carry_scr[:, N_STATE:N_STATE + half] = c[1]
    carry_scr[:, half:N_STATE] = c[2]
    carry_scr[:, N_STATE + half:2 * N_STATE] = c[3]
    sfin_ref[0] = carry_scr[...]

    y = _dot(x_scr[...].astype(BF16), cmat_ref[...]) + d_ref[...] * u
    z = _dot(_gelu_tanh(y).astype(BF16), wglu_ref[...])
    ya_ref[0] = (z[:, 0:SSM_WIDTH] * _sigmoid(z[:, SSM_WIDTH:])).astype(ya_ref.dtype)

    ub = uav_ref[0, :, SSM_WIDTH:SSM_WIDTH + SGU_WIDTH]
    vb = uav_ref[0, :, SSM_WIDTH + SGU_WIDTH:MIX_WIDTH]
    xc = vb - jnp.mean(vb, axis=-1, keepdims=True)
    vn = xc * lax.rsqrt(jnp.mean(xc * xc, axis=-1, keepdims=True) + EPS) * lng_ref[...]
    vn_ref[0] = vn
    vnb = vn.astype(BF16)
    row = lax.broadcasted_iota(jnp.int32, (n, n), 0)
    col = lax.broadcasted_iota(jnp.int32, (n, n), 1)
    group = lax.broadcasted_iota(jnp.int32, (n, SGU_WIDTH), 1) // SGU_GROUP_DIM
    w_tril = [jnp.where(row >= col, wsgu_ref[g], 0.0).astype(BF16) for g in range(SGU_GROUPS)]
    for ch in range(tb // n):
        rows = slice(ch * n, (ch + 1) * n)
        mixed = bsgu_ref[...]
        for g in range(SGU_GROUPS):
            mixed = mixed + jnp.where(group == g, _dot(w_tril[g], vnb[rows]), 0.0)
        yb_ref[0, rows] = (ub[rows] * mixed).astype(yb_ref.dtype)


def _mixab(uav, s0, lp, tb, n):
    bsz, L, _ = uav.shape
    grid = (bsz, L // tb)
    tok = lambda width, dtype: (pl.BlockSpec((1, tb, width), lambda b, i: (b, i, 0)),
                                jax.ShapeDtypeStruct((bsz, L, width), dtype))
    ya_spec, ya_shape = tok(SSM_WIDTH, BF16)
    yb_spec, yb_shape = tok(SGU_WIDTH, BF16)
    vn_spec, vn_shape = tok(SGU_WIDTH, F32)
    state_spec = pl.BlockSpec((1, 1, 2 * N_STATE), lambda b, i: (b, 0, 0))
    return pl.pallas_call(
        functools.partial(_mixab_kernel, tb=tb, n=n),
        grid=grid,
        in_specs=[pl.BlockSpec((1, tb, MIX_WIDTH), lambda b, i: (b, i, 0)),
                  state_spec,
                  _resident(lp["bmat"].shape),
                  _resident(lp["tab"].shape), _resident(lp["cmat"].shape),
                  _resident(lp["d"].shape), _resident(lp["w_glu"].shape),
                  _resident(lp["sgu_norm"].shape), _resident((SGU_GROUPS, n, n)),
                  _resident((n, SGU_WIDTH))],
        out_specs=[ya_spec, yb_spec, vn_spec, state_spec],
        out_shape=[ya_shape, yb_shape, vn_shape,
                   jax.ShapeDtypeStruct((bsz, 1, 2 * N_STATE), F32)],
        scratch_shapes=[pltpu.VMEM((tb, 2 * N_STATE), F32), pltpu.VMEM((1, 2 * N_STATE), F32)],
        compiler_params=_params(("parallel", "arbitrary")),
        name="mixab",
    )(uav, s0, lp["bmat"], lp["tab"], lp["cmat"], lp["d"], lp["w_glu"],
      lp["sgu_norm"], lp["sgu_w"][:, :n, :n], lp["sgu_bias"][:n])


def _attn_kernel(q_ref, kd_ref, vd_ref, kp_ref, vp_ref, kh_hbm, vh_hbm, o_ref, kbuf, vbuf, sem, *,
                 layer, tq, tk, hist0, hist_per_q):
    b = pl.program_id(0)
    i = pl.program_id(1)
    n_hist = hist0 + i * hist_per_q
    heads = range(SB_HEADS)
    qs = [(q_ref[0, h] * (SB_HEAD_DIM ** -0.5)).astype(BF16) for h in heads]

    def suffix_ones(size):
        r = lax.broadcasted_iota(jnp.int32, (size, size), 0)
        c = lax.broadcasted_iota(jnp.int32, (size, size), 1)
        return jnp.where(r >= c, 1.0, 0.0).astype(BF16)

    def sweep(q, kb, vb, ones, visible, surv, acc):
        z = lax.dot_general(q, kb, (((1,), (1,)), ((), ())), preferred_element_type=F32)
        log_1m = -(jnp.maximum(z, 0.0) + jnp.log(1.0 + jnp.exp(-jnp.abs(z))))
        if visible is not None:
            log_1m = jnp.where(visible, log_1m, 0.0)
        hi, lo = _split_bf16(log_1m)
        suffix = _dot(hi, ones) + _dot(lo, ones)
        wts = jnp.exp(z + suffix + surv)
        if visible is not None:
            wts = jnp.where(visible, wts, 0.0)
        acc = acc + _dot(wts.astype(BF16), vb)
        return surv + suffix[:, 0:1], acc

    r = lax.broadcasted_iota(jnp.int32, (tq, tq), 0)
    c = lax.broadcasted_iota(jnp.int32, (tq, tq), 1)
    ones_q = suffix_ones(tq)
    ones_k = ones_q if tk == tq else suffix_ones(tk)
    state = tuple(
        sweep(qs[h], kd_ref[0, 0, h].astype(BF16), vd_ref[0, 0, h].astype(BF16), ones_q, c < r,
              jnp.zeros((tq, 1), F32), jnp.zeros((tq, SB_HEAD_DIM), F32))
        for h in heads)

    def prev_block(st):
        return tuple(sweep(qs[h], kp_ref[0, 0, h].astype(BF16), vp_ref[0, 0, h].astype(BF16),
                           ones_k, None, *st[h]) for h in heads)

    state = lax.cond(n_hist > 0, prev_block, lambda st: st, state)

    def fetch(j):
        rows = pl.ds(pl.multiple_of(j * tk, tk), tk)
        return (pltpu.make_async_copy(kh_hbm.at[layer, b, :, rows, :], kbuf, sem.at[0]),
                pltpu.make_async_copy(vh_hbm.at[layer, b, :, rows, :], vbuf, sem.at[1]))

    def more(carry):
        j, st = carry
        alive = st[0][0]
        for h in heads[1:]:
            alive = jnp.maximum(alive, st[h][0])
        return jnp.logical_and(j >= 0, jnp.max(alive) > EXP_UNDERFLOW)

    def step(carry):
        j, st = carry
        copies = fetch(j)
        for cp in copies:
            cp.start()
        for cp in copies:
            cp.wait()
        st = tuple(sweep(qs[h], kbuf[h].astype(BF16), vbuf[h].astype(BF16), ones_k, None, *st[h])
                   for h in heads)
        return j - 1, st

    _, state = lax.while_loop(more, step, (n_hist - 2, state))
    o_ref[0] = jnp.concatenate([state[h][1] for h in heads], axis=1).astype(o_ref.dtype)


def _attention(q, k_new, v_new, k_hist, v_hist, layer, tq, tk, hist0, hist_per_q):
    bsz, H, L, Dh = q.shape
    prev = lambda b, i: (layer, b, 0, jnp.maximum(hist0 + i * hist_per_q - 1, 0), 0)
    diag_spec = pl.BlockSpec((1, 1, H, tq, Dh), lambda b, i: (layer, b, 0, i, 0))
    prev_spec = pl.BlockSpec((1, 1, H, tk, Dh), prev)
    any_spec = pl.BlockSpec(memory_space=pl.ANY)
    return pl.pallas_call(
        functools.partial(_attn_kernel, layer=layer, tq=tq, tk=tk, hist0=hist0,
                          hist_per_q=hist_per_q),
        grid=(bsz, L // tq),
        in_specs=[pl.BlockSpec((1, H, tq, Dh), lambda b, i: (b, 0, i, 0)),
                  diag_spec, diag_spec, prev_spec, prev_spec, any_spec, any_spec],
        out_specs=pl.BlockSpec((1, tq, H * Dh), lambda b, i: (b, i, 0)),
        out_shape=jax.ShapeDtypeStruct((bsz, L, H * Dh), BF16),
        scratch_shapes=[pltpu.VMEM((H, tk, Dh), F32), pltpu.VMEM((H, tk, Dh), F32),
                        pltpu.SemaphoreType.DMA((2,))],
        compiler_params=_params(("parallel", "arbitrary")),
        name="stickbreak",
    )(q, k_new, v_new, k_hist, v_hist, k_hist, v_hist)


def _merge_kernel(h_ref, g_ref, wg_ref, ya_ref, yb_ref, yc_ref, wa_ref, wb_ref, wc_ref, wout_ref,
                  hout_ref, *, nb, tl):
    m = nb * tl
    x = h_ref[...].reshape(m, D_MODEL)
    xn = _rms(x, g_ref[...]).astype(BF16)

    def gate(idx):
        return _sigmoid(_dot(xn, wg_ref[:, idx * D_MODEL:(idx + 1) * D_MODEL]))

    merged = gate(0) * _dot(ya_ref[...].reshape(m, SSM_WIDTH), wa_ref[...])
    merged = merged + gate(1) * _dot(yb_ref[...].reshape(m, SGU_WIDTH), wb_ref[...])
    merged = merged + gate(2) * _dot(yc_ref[...].reshape(m, SB_WIDTH), wc_ref[...])
    hout_ref[...] = (x + _dot(merged.astype(BF16), wout_ref[...])).reshape(nb, tl, D_MODEL)


def _merge(h, ya, yb, yc, lp, nb, tl):
    bsz, L, _ = h.shape
    tok = lambda width: pl.BlockSpec((nb, tl, width), lambda b, i: (b, i, 0))
    return pl.pallas_call(
        functools.partial(_merge_kernel, nb=nb, tl=tl),
        grid=(bsz // nb, L // tl),
        in_specs=[tok(D_MODEL), _resident((1, D_MODEL)), _resident(lp["w_gates"].shape),
                  tok(SSM_WIDTH), tok(SGU_WIDTH), tok(SB_WIDTH),
                  _resident(lp["w_branch_a"].shape), _resident(lp["w_branch_b"].shape),
                  _resident(lp["w_branch_c"].shape), _resident(lp["w_out"].shape)],
        out_specs=tok(D_MODEL),
        out_shape=jax.ShapeDtypeStruct(h.shape, F32),
        compiler_params=_params(("parallel", "parallel")),
        name="merge",
    )(h, lp["norm_mix"], lp["w_gates"], ya, yb, yc, lp["w_branch_a"], lp["w_branch_b"],
      lp["w_branch_c"], lp["w_out"])


def _ffn_kernel(h_ref, g_ref, wgu_ref, wd_ref, gfin_ref, out_ref, act_scr, *, final):
    x = h_ref[...]
    xn = _rms(x, g_ref[...]).astype(BF16)
    for c0 in range(0, FFN_HIDDEN, FFN_CHUNK):
        gate = _dot(xn, wgu_ref[:, c0:c0 + FFN_CHUNK])
        up = _dot(xn, wgu_ref[:, FFN_HIDDEN + c0:FFN_HIDDEN + c0 + FFN_CHUNK])
        act_scr[:, c0:c0 + FFN_CHUNK] = (gate * _sigmoid(gate) * up).astype(BF16)
    y = x + _dot(act_scr[...], wd_ref[...])
    if final:
        y = _rms(y, gfin_ref[...])
    out_ref[...] = y


def _ffn(h2d, lp, g_final, final, tm):
    T = h2d.shape[0]
    tok = pl.BlockSpec((tm, D_MODEL), lambda i: (i, 0))
    return pl.pallas_call(
        functools.partial(_ffn_kernel, final=final),
        grid=(T // tm,),
        in_specs=[tok, _resident((1, D_MODEL)), _resident(lp["w_gate_up"].shape),
                  _resident(lp["w_down"].shape), _resident((1, D_MODEL))],
        out_specs=tok,
        out_shape=jax.ShapeDtypeStruct(h2d.shape, F32),
        scratch_shapes=[pltpu.VMEM((tm, FFN_HIDDEN), BF16)],
        compiler_params=_params(("parallel",)),
        name="ffn",
    )(h2d, lp["norm_ffn"], lp["w_gate_up"], lp["w_down"], g_final)


def _cmul(a, b):
    return a[0] * b[0] - a[1] * b[1], a[0] * b[1] + a[1] * b[0]


def _s5_tables(a_re, a_im, b_re, b_im, c_re, c_im, log_dt):
    dt = jnp.exp(log_dt)[:, None]
    mag = jnp.exp(a_re * dt)
    ab = (mag * jnp.cos(a_im * dt), mag * jnp.sin(a_im * dt))
    num_re, num_im = ab[0] - 1.0, ab[1]
    den = a_re * a_re + a_im * a_im
    f_re = (num_re * a_re + num_im * a_im) / den
    f_im = (num_im * a_re - num_re * a_im) / den
    bb_re = f_re[..., None] * b_re - f_im[..., None] * b_im
    bb_im = f_re[..., None] * b_im + f_im[..., None] * b_re
    eye = jnp.eye(SSM_GROUPS, dtype=F32)
    bd_in = lambda m: jnp.einsum("gpc,gh->gchp", m, eye).reshape(SSM_WIDTH, N_STATE)
    bmat = jnp.concatenate([bd_in(bb_re), bd_in(bb_im)], axis=1).astype(BF16)
    bd_out = lambda m: jnp.einsum("gcp,gh->gphc", m, eye).reshape(N_STATE, SSM_WIDTH)
    cmat = jnp.concatenate([bd_out(c_re), -bd_out(c_im)], axis=0).astype(BF16)
    a1 = (ab[0].reshape(1, N_STATE), ab[1].reshape(1, N_STATE))
    pows = [a1]
    for _ in range(SUBLANES - 1):
        pows.append(_cmul(pows[-1], a1))
    row = jnp.arange(SUBLANES)[:, None]
    tab = []
    for k in range(3):
        p = pows[(1 << k) - 1]
        keep = row >= (1 << k)
        tab += [jnp.where(keep, p[0], 0.0), jnp.where(keep, p[1], 0.0)]
    tab += [jnp.concatenate([p[0] for p in pows], axis=0),
            jnp.concatenate([p[1] for p in pows], axis=0)]
    return bmat, cmat, jnp.stack(tab)


def _layer_params(i, norm_mix, w_in, ssm_a_re, ssm_a_im, ssm_b_re, ssm_b_im, ssm_c_re, ssm_c_im,
                  ssm_d, ssm_log_dt, ssm_w_glu, sgu_norm, sgu_w, sgu_b,
                  w_branch_a, w_branch_b, w_branch_c, w_out, norm_ffn, w_gate_up, w_down):
    bmat, cmat, tab = _s5_tables(ssm_a_re[i], ssm_a_im[i], ssm_b_re[i], ssm_b_im[i],
                                 ssm_c_re[i], ssm_c_im[i], ssm_log_dt[i])
    w_in_b = w_in[i].astype(BF16)
    return {
        "norm_mix": norm_mix[i].reshape(1, D_MODEL),
        "w_mix": w_in_b[:, :GATE_OFFSET],
        "w_gates": w_in_b[:, GATE_OFFSET:],
        "bmat": bmat, "cmat": cmat, "tab": tab,
        "d": ssm_d[i].reshape(1, SSM_WIDTH),
        "w_glu": ssm_w_glu[i].astype(BF16),
        "sgu_norm": sgu_norm[i].reshape(1, SGU_WIDTH),
        "sgu_w": sgu_w[i],
        "sgu_bias": jnp.repeat(sgu_b[i].T, SGU_GROUP_DIM, axis=1),
        "w_branch_a": w_branch_a[i].astype(BF16),
        "w_branch_b": w_branch_b[i].astype(BF16),
        "w_branch_c": w_branch_c[i].astype(BF16),
        "w_out": w_out[i].astype(BF16),
        "norm_ffn": norm_ffn[i].reshape(1, D_MODEL),
        "w_gate_up": w_gate_up[i].astype(BF16),
        "w_down": w_down[i].astype(BF16),
    }


def _layer(h, s0, k_stack, v_stack, k_cache, v_cache, lp, g_final, layer, depth):
    bsz, L, _ = h.shape
    if k_cache is None:
        nb, tl = 1, min(L, TOKEN_TILE)
        tb, n = tl, min(L, SGU_CHUNK)
    else:
        nb, tl = bsz, L
        tb, n = L, L
    uav, q, k_stack, v_stack = _inproj(h, lp["norm_mix"], lp["w_mix"], k_stack, v_stack,
                                       layer, depth, nb, tl)
    ya, yb, vn, sfin = _mixab(uav, s0, lp, tb, n)
    if k_cache is None:
        tq = min(L, ATTN_BLOCK)
        yc = _attention(q, k_stack, v_stack, k_stack, v_stack, layer, tq, tq, 0, 1)
    else:
        past = k_cache.shape[3]
        tk = min(past, ATTN_BLOCK)
        yc = _attention(q, k_stack, v_stack, k_cache, v_cache, layer, L, tk, past // tk, 0)
    h1 = _merge(h, ya, yb, yc, lp, nb, tl)
    T = bsz * L
    h2 = _ffn(h1.reshape(T, D_MODEL), lp, g_final, layer == depth - 1, min(T, TOKEN_TILE))
    return h2.reshape(bsz, L, D_MODEL), sfin, k_stack, v_stack, vn


def kernel(x_prompt, x_sample, state_ssm_re, state_ssm_im, cache_sb_k, cache_sb_v, norm_mix, w_in, ssm_a_re, ssm_a_im, ssm_b_re, ssm_b_im, ssm_c_re, ssm_c_im, ssm_d, ssm_log_dt, ssm_w_glu, sgu_norm, sgu_w, sgu_b, w_branch_a, w_branch_b, w_branch_c, w_out, norm_ffn, w_gate_up, w_down, norm_final):
    weights = (norm_mix, w_in, ssm_a_re, ssm_a_im, ssm_b_re, ssm_b_im, ssm_c_re, ssm_c_im,
               ssm_d, ssm_log_dt, ssm_w_glu, sgu_norm, sgu_w, sgu_b,
               w_branch_a, w_branch_b, w_branch_c, w_out, norm_ffn, w_gate_up, w_down)
    depth = w_in.shape[0]
    bp, bs = x_prompt.shape[0], x_sample.shape[0]
    g_final = norm_final.reshape(1, D_MODEL)
    zero_state = jnp.zeros((bp, 1, 2 * N_STATE), F32)
    hp, hs = x_prompt, x_sample
    kp = vp = ks = vs = None
    outs = [[] for _ in range(5)]
    for i in range(depth):
        lp = _layer_params(i, *weights)
        hp, sfin_p, kp, vp, _ = _layer(hp, zero_state, kp, vp, None, None, lp, g_final, i, depth)
        s0 = jnp.concatenate([state_ssm_re[i].reshape(bs, 1, N_STATE),
                              state_ssm_im[i].reshape(bs, 1, N_STATE)], axis=-1)
        hs, sfin_s, ks, vs, vn_s = _layer(hs, s0, ks, vs, cache_sb_k, cache_sb_v, lp, g_final,
                                          i, depth)
        state = lambda s, part: s[:, 0, part * N_STATE:(part + 1) * N_STATE].reshape(
            -1, SSM_GROUPS, SSM_STATE)
        for lst, val in zip(outs, (state(sfin_p, 0), state(sfin_p, 1),
                                   state(sfin_s, 0), state(sfin_s, 1), vn_s)):
            lst.append(val)
    p_re, p_im, s_re, s_im, s_vb = (jnp.stack(lst) for lst in outs)
    return (hp, hs, p_re, p_im, kp, vp, s_re, s_im, ks, vs, s_vb)
```

```python
import functools
import math

import jax
import jax.numpy as jnp
from jax import lax
from jax.experimental import pallas as pl
from jax.experimental.pallas import tpu as pltpu

F32 = jnp.float32
BF16 = jnp.bfloat16

D_MODEL = 1024
SSM_WIDTH = 256
SSM_GROUP = 16
SSM_GROUPS = 16
SSM_STATE = 64
N_STATE = SSM_GROUPS * SSM_STATE
SGU_WIDTH = 256
SGU_CHUNK = 128
SGU_GROUPS = 4
SGU_GROUP_DIM = SGU_WIDTH // SGU_GROUPS
SB_HEADS = 8
SB_HEAD_DIM = 64
SB_WIDTH = SB_HEADS * SB_HEAD_DIM
FFN_HIDDEN = 2816
MIX_WIDTH = SSM_WIDTH + 2 * SGU_WIDTH
QKV_OFFSET = MIX_WIDTH
KV_OFFSET = MIX_WIDTH + SB_WIDTH
GATE_OFFSET = MIX_WIDTH + 3 * SB_WIDTH
EPS = 1e-6

SUBLANES = 8
VMEM_LIMIT = 56 * 1024 * 1024
QK_SCALE = SB_HEAD_DIM ** -0.5 * math.log2(math.e)
DEAD_BITS = 100.0 * math.log2(math.e)
MASKED_SCORE = -1e30
ATTN_BLOCK = 256
FFN_CHUNK = 256
TOKEN_TILE = 512


def _rms(x, g):
    return x * lax.rsqrt(jnp.mean(x * x, axis=-1, keepdims=True) + EPS) * g


def _dot(a, b):
    return jnp.dot(a, b, preferred_element_type=F32)


def _sigmoid(x):
    return 1.0 / (1.0 + jnp.exp(-x))


def _split_bf16(x):
    hi = x.astype(BF16)
    lo = (x - hi.astype(F32)).astype(BF16)
    return hi, lo


def _params(semantics):
    return pltpu.CompilerParams(dimension_semantics=semantics, vmem_limit_bytes=VMEM_LIMIT)


def _resident(shape):
    zeros = (0,) * len(shape)
    return pl.BlockSpec(shape, lambda *_: zeros, pipeline_mode=pl.Buffered(1))


def _inproj_kernel(*refs, nb, tl, transposed, fresh):
    if fresh:
        h_ref, g_ref, w_ref, wkv_ref, uav_ref, q_ref, k_ref, v_ref = refs
    else:
        h_ref, g_ref, w_ref, wkv_ref, _, _, uav_ref, q_ref, k_ref, v_ref = refs
    x = h_ref[...].reshape(nb * tl, D_MODEL)
    xn = _rms(x, g_ref[...]).astype(BF16)
    uav_ref[...] = _dot(xn, w_ref[:, 0:MIX_WIDTH]).reshape(nb, tl, MIX_WIDTH)
    q = _dot(xn, w_ref[:, QKV_OFFSET:QKV_OFFSET + SB_WIDTH]) * QK_SCALE
    q_ref[...] = q.astype(BF16).reshape(nb, tl, SB_WIDTH)
    dh = SB_HEAD_DIM
    for idx, ref in enumerate((k_ref, v_ref)):
        if transposed:
            seg = lax.dot_general(wkv_ref[idx * SB_WIDTH:(idx + 1) * SB_WIDTH, :], xn,
                                  (((1,), (1,)), ((), ())), preferred_element_type=F32)
        else:
            seg = _dot(xn, wkv_ref[:, idx * SB_WIDTH:(idx + 1) * SB_WIDTH])
        for bb in range(nb):
            for hh in range(SB_HEADS):
                if transposed:
                    ref[0, bb, hh] = seg[hh * dh:(hh + 1) * dh, bb * tl:(bb + 1) * tl]
                else:
                    ref[0, bb, hh] = seg[bb * tl:(bb + 1) * tl, hh * dh:(hh + 1) * dh]
        if fresh:
            ref[1:] = jnp.zeros((ref.shape[0] - 1,) + ref.shape[1:], F32)


def _inproj(h, g, w_mix, w_kv, k_stack, v_stack, layer, depth, nb, tl, transposed):
    bsz, L, _ = h.shape
    fresh = k_stack is None
    minor = (SB_HEAD_DIM, L) if transposed else (L, SB_HEAD_DIM)
    minor_blk = (SB_HEAD_DIM, tl) if transposed else (tl, SB_HEAD_DIM)
    stack_shape = jax.ShapeDtypeStruct((depth, bsz, SB_HEADS) + minor, F32)
    pos = (lambda i: (0, i)) if transposed else (lambda i: (i, 0))
    first = 0 if fresh else layer
    stack_spec = pl.BlockSpec((depth if fresh else 1, nb, SB_HEADS) + minor_blk,
                              lambda b, i: (first, b, 0) + pos(i))
    tok = lambda width: pl.BlockSpec((nb, tl, width), lambda b, i: (b, i, 0))
    in_specs = [tok(D_MODEL), _resident((1, D_MODEL)), _resident(w_mix.shape),
                _resident(w_kv.shape)]
    args = [h, g, w_mix, w_kv]
    aliases = {}
    if not fresh:
        in_specs += [pl.BlockSpec(memory_space=pl.ANY)] * 2
        args += [k_stack, v_stack]
        aliases = {4: 2, 5: 3}
    return pl.pallas_call(
        functools.partial(_inproj_kernel, nb=nb, tl=tl, transposed=transposed, fresh=fresh),
        grid=(bsz // nb, L // tl),
        in_specs=in_specs,
        out_specs=[tok(MIX_WIDTH), tok(SB_WIDTH), stack_spec, stack_spec],
        out_shape=[jax.ShapeDtypeStruct((bsz, L, MIX_WIDTH), F32),
                   jax.ShapeDtypeStruct((bsz, L, SB_WIDTH), BF16), stack_shape, stack_shape],
        input_output_aliases=aliases,
        compiler_params=_params(("parallel", "parallel")),
        name="inproj",
    )(*args)


def _gelu_tanh(y):
    c = math.sqrt(2.0 / math.pi)
    return 0.5 * y * (1.0 + jnp.tanh(c * (y + 0.044715 * (y * y * y))))


def _mixab_kernel(uav_ref, s0_ref, bmat_ref, tab_ref, cmat_ref, d_ref, wglu_ref,
                  lng_ref, wsgu_ref, bsgu_ref,
                  ya_ref, yb_ref, vn_ref, sfin_ref,
                  x_scr, carry_scr, *, tb, n):
    i = pl.program_id(1)

    @pl.when(i == 0)
    def _():
        carry_scr[...] = s0_ref[0]

    u = uav_ref[0, :, 0:SSM_WIDTH]
    x_scr[...] = _dot(u.astype(BF16), bmat_ref[...])

    half = N_STATE // 2

    def tile_scan(j, carry):
        r0 = pl.multiple_of(j * SUBLANES, SUBLANES)
        new_carry = []
        for hf in range(2):
            re_cols = pl.ds(hf * half, half)
            im_cols = pl.ds(N_STATE + hf * half, half)
            xr = x_scr[pl.ds(r0, SUBLANES), re_cols]
            xi = x_scr[pl.ds(r0, SUBLANES), im_cols]
            for k in range(3):
                sr = pltpu.roll(xr, 1 << k, 0)
                si = pltpu.roll(xi, 1 << k, 0)
                ar = tab_ref[2 * k, :, re_cols]
                ai = tab_ref[2 * k + 1, :, re_cols]
                xr, xi = xr + (ar * sr - ai * si), xi + (ar * si + ai * sr)
            cr, ci = carry[2 * hf], carry[2 * hf + 1]
            pr = tab_ref[6, :, re_cols]
            pi_ = tab_ref[7, :, re_cols]
            xr, xi = xr + (pr * cr - pi_ * ci), xi + (pr * ci + pi_ * cr)
            x_scr[pl.ds(r0, SUBLANES), re_cols] = xr
            x_scr[pl.ds(r0, SUBLANES), im_cols] = xi
            new_carry += [xr[SUBLANES - 1:SUBLANES], xi[SUBLANES - 1:SUBLANES]]
        return tuple(new_carry)

    carry0 = (carry_scr[:, 0:half], carry_scr[:, N_STATE:N_STATE + half],
              carry_scr[:, half:N_STATE], carry_scr[:, N_STATE + half:2 * N_STATE])
    c = lax.fori_loop(0, tb // SUBLANES, tile_scan, carry0)
    carry_scr[:, 0:half] = c[0]
    carry_scr[:, N_STATE:N_STATE + half] = c[1]
    carry_scr[:, half:N_STATE] = c[2]
    carry_scr[:, N_STATE + half:2 * N_STATE] = c[3]
    sfin_ref[0] = carry_scr[...]

    y = _dot(x_scr[...].astype(BF16), cmat_ref[...]) + d_ref[...] * u
    z = _dot(_gelu_tanh(y).astype(BF16), wglu_ref[...])
    ya_ref[0] = (z[:, 0:SSM_WIDTH] * _sigmoid(z[:, SSM_WIDTH:])).astype(ya_ref.dtype)

    ub = uav_ref[0, :, SSM_WIDTH:SSM_WIDTH + SGU_WIDTH]
    vb = uav_ref[0, :, SSM_WIDTH + SGU_WIDTH:MIX_WIDTH]
    xc = vb - jnp.mean(vb, axis=-1, keepdims=True)
    vn = xc * lax.rsqrt(jnp.mean(xc * xc, axis=-1, keepdims=True) + EPS) * lng_ref[...]
    vn_ref[0] = vn
    vnb = vn.astype(BF16)
    row = lax.broadcasted_iota(jnp.int32, (n, n), 0)
    col = lax.broadcasted_iota(jnp.int32, (n, n), 1)
    group = lax.broadcasted_iota(jnp.int32, (n, SGU_WIDTH), 1) // SGU_GROUP_DIM
    w_tril = [jnp.where(row >= col, wsgu_ref[g], 0.0).astype(BF16) for g in range(SGU_GROUPS)]
    for ch in range(tb // n):
        rows = slice(ch * n, (ch + 1) * n)
        mixed = bsgu_ref[...]
        for g in range(SGU_GROUPS):
            mixed = mixed + jnp.where(group == g, _dot(w_tril[g], vnb[rows]), 0.0)
        yb_ref[0, rows] = (ub[rows] * mixed).astype(yb_ref.dtype)


def _mixab(uav, s0, lp, tb, n):
    bsz, L, _ = uav.shape
    grid = (bsz, L // tb)
    tok = lambda width, dtype: (pl.BlockSpec((1, tb, width), lambda b, i: (b, i, 0)),
                                jax.ShapeDtypeStruct((bsz, L, width), dtype))
    ya_spec, ya_shape = tok(SSM_WIDTH, BF16)
    yb_spec, yb_shape = tok(SGU_WIDTH, BF16)
    vn_spec, vn_shape = tok(SGU_WIDTH, F32)
    state_spec = pl.BlockSpec((1, 1, 2 * N_STATE), lambda b, i: (b, 0, 0))
    return pl.pallas_call(
        functools.partial(_mixab_kernel, tb=tb, n=n),
        grid=grid,
        in_specs=[pl.BlockSpec((1, tb, MIX_WIDTH), lambda b, i: (b, i, 0)),
                  state_spec,
                  _resident(lp["bmat"].shape),
                  _resident(lp["tab"].shape), _resident(lp["cmat"].shape),
                  _resident(lp["d"].shape), _resident(lp["w_glu"].shape),
                  _resident(lp["sgu_norm"].shape), _resident((SGU_GROUPS, n, n)),
                  _resident((n, SGU_WIDTH))],
        out_specs=[ya_spec, yb_spec, vn_spec, state_spec],
        out_shape=[ya_shape, yb_shape, vn_shape,
                   jax.ShapeDtypeStruct((bsz, 1, 2 * N_STATE), F32)],
        scratch_shapes=[pltpu.VMEM((tb, 2 * N_STATE), F32), pltpu.VMEM((1, 2 * N_STATE), F32)],
        compiler_params=_params(("parallel", "arbitrary")),
        name="mixab",
    )(uav, s0, lp["bmat"], lp["tab"], lp["cmat"], lp["d"], lp["w_glu"],
      lp["sgu_norm"], lp["sgu_w"][:, :n, :n], lp["sgu_bias"][:n])


def _attn_kernel(q_ref, kd_ref, vd_ref, kp_ref, vp_ref, kh_hbm, vh_hbm, o_ref, kbuf, vbuf, sem, *,
                 layer, tq, tk, hist0, hist_per_q, diag_transposed):
    b = pl.program_id(0)
    i = pl.program_id(1)
    n_hist = hist0 + i * hist_per_q
    heads = range(SB_HEADS)
    dh = SB_HEAD_DIM
    nt = (((1,), (1,)), ((), ()))
    qs = [q_ref[0, :, h * dh:(h + 1) * dh] for h in heads]

    def suffix_ones(size):
        r = lax.broadcasted_iota(jnp.int32, (size, size), 0)
        c = lax.broadcasted_iota(jnp.int32, (size, size), 1)
        return jnp.where(r >= c, 1.0, 0.0).astype(BF16)

    def decay(z2, ones):
        sp = jnp.maximum(z2, jnp.log2(1.0 + jnp.exp2(jnp.minimum(z2, 126.0))))
        hi, lo = _split_bf16(sp)
        return _dot(hi, ones) + _dot(lo, ones)

    def weights(z2, dec, dead):
        return jnp.exp2(z2 - (dec if dead is None else dec + dead)).astype(BF16)

    def history(k_t, v_t, ones, deads, accs, scale):
        zs = [_dot(qs[h], k_t[h]) for h in heads]
        decs = [decay(zs[h], ones) for h in heads]
        new_deads, new_accs = [], []
        for h in heads:
            pv = lax.dot_general(weights(zs[h], decs[h], deads[h]), v_t[h], nt,
                                 preferred_element_type=F32)
            total = decs[h][:, 0:1]
            if scale is not None:
                pv, total = scale * pv, scale * total
            new_accs.append(accs[h] + pv)
            new_deads.append(deads[h] + total)
        return tuple(new_deads), tuple(new_accs)

    r = lax.broadcasted_iota(jnp.int32, (tq, tq), 0)
    c = lax.broadcasted_iota(jnp.int32, (tq, tq), 1)
    ones_q = suffix_ones(tq)
    ones_k = ones_q if tk == tq else suffix_ones(tk)
    if diag_transposed:
        zd = [_dot(qs[h], kd_ref[0, 0, h].astype(BF16)) for h in heads]
    else:
        zd = [lax.dot_general(qs[h], kd_ref[0, 0, h].astype(BF16), nt, preferred_element_type=F32)
              for h in heads]
    zd = [jnp.where(c < r, z, MASKED_SCORE) for z in zd]
    dd = [decay(z, ones_q) for z in zd]
    accs = []
    for h in heads:
        w = weights(zd[h], dd[h], None)
        vd = vd_ref[0, 0, h].astype(BF16)
        accs.append(lax.dot_general(w, vd, nt, preferred_element_type=F32) if diag_transposed
                    else _dot(w, vd))
    deads = tuple(d[:, 0:1] for d in dd)

    has_prev = (n_hist > 0).astype(F32)
    deads, accs = history([kp_ref[0, 0, h].astype(BF16) for h in heads],
                          [vp_ref[0, 0, h].astype(BF16) for h in heads],
                          ones_k, deads, tuple(accs), has_prev)

    def fetch(j):
        cols = pl.ds(pl.multiple_of(j * tk, tk), tk)
        return (pltpu.make_async_copy(kh_hbm.at[layer, b, :, :, cols], kbuf, sem.at[0]),
                pltpu.make_async_copy(vh_hbm.at[layer, b, :, :, cols], vbuf, sem.at[1]))

    def more(carry):
        j, deads, _ = carry
        least = deads[0]
        for h in heads[1:]:
            least = jnp.minimum(least, deads[h])
        return jnp.logical_and(j >= 0, jnp.min(least) < DEAD_BITS)

    def step(carry):
        j, deads, accs = carry
        copies = fetch(j)
        for cp in copies:
            cp.start()
        for cp in copies:
            cp.wait()
        deads, accs = history([kbuf[h].astype(BF16) for h in heads],
                              [vbuf[h].astype(BF16) for h in heads], ones_k, deads, accs, None)
        return j - 1, deads, accs

    _, _, accs = lax.while_loop(more, step, (n_hist - 2, deads, accs))
    o_ref[0] = jnp.concatenate(accs, axis=1).astype(o_ref.dtype)


def _attention(q, k_new, v_new, k_hist, v_hist, layer, tq, tk, hist0, hist_per_q, diag_transposed):
    bsz, L, _ = q.shape
    H, Dh = SB_HEADS, SB_HEAD_DIM
    if diag_transposed:
        diag_spec = pl.BlockSpec((1, 1, H, Dh, tq), lambda b, i: (layer, b, 0, 0, i))
    else:
        diag_spec = pl.BlockSpec((1, 1, H, tq, Dh), lambda b, i: (layer, b, 0, i, 0))
    prev_spec = pl.BlockSpec(
        (1, 1, H, Dh, tk),
        lambda b, i: (layer, b, 0, 0, jnp.maximum(hist0 + i * hist_per_q - 1, 0)))
    any_spec = pl.BlockSpec(memory_space=pl.ANY)
    tok = pl.BlockSpec((1, tq, H * Dh), lambda b, i: (b, i, 0))
    return pl.pallas_call(
        functools.partial(_attn_kernel, layer=layer, tq=tq, tk=tk, hist0=hist0,
                          hist_per_q=hist_per_q, diag_transposed=diag_transposed),
        grid=(bsz, L // tq),
        in_specs=[tok, diag_spec, diag_spec, prev_spec, prev_spec, any_spec, any_spec],
        out_specs=tok,
        out_shape=jax.ShapeDtypeStruct((bsz, L, H * Dh), BF16),
        scratch_shapes=[pltpu.VMEM((H, Dh, tk), F32), pltpu.VMEM((H, Dh, tk), F32),
                        pltpu.SemaphoreType.DMA((2,))],
        compiler_params=_params(("parallel", "arbitrary")),
        name="stickbreak",
    )(q, k_new, v_new, k_hist, v_hist, k_hist, v_hist)


def _merge_kernel(h_ref, g_ref, wg_ref, ya_ref, yb_ref, yc_ref, wa_ref, wb_ref, wc_ref, wout_ref,
                  hout_ref, *, nb, tl):
    m = nb * tl
    x = h_ref[...].reshape(m, D_MODEL)
    xn = _rms(x, g_ref[...]).astype(BF16)

    def gate(idx):
        return _sigmoid(_dot(xn, wg_ref[:, idx * D_MODEL:(idx + 1) * D_MODEL]))

    merged = gate(0) * _dot(ya_ref[...].reshape(m, SSM_WIDTH), wa_ref[...])
    merged = merged + gate(1) * _dot(yb_ref[...].reshape(m, SGU_WIDTH), wb_ref[...])
    merged = merged + gate(2) * _dot(yc_ref[...].reshape(m, SB_WIDTH), wc_ref[...])
    hout_ref[...] = (x + _dot(merged.astype(BF16), wout_ref[...])).reshape(nb, tl, D_MODEL)


def _merge(h, ya, yb, yc, lp, nb, tl):
    bsz, L, _ = h.shape
    tok = lambda width: pl.BlockSpec((nb, tl, width), lambda b, i: (b, i, 0))
    return pl.pallas_call(
        functools.partial(_merge_kernel, nb=nb, tl=tl),
        grid=(bsz // nb, L // tl),
        in_specs=[tok(D_MODEL), _resident((1, D_MODEL)), _resident(lp["w_gates"].shape),
                  tok(SSM_WIDTH), tok(SGU_WIDTH), tok(SB_WIDTH),
                  _resident(lp["w_branch_a"].shape), _resident(lp["w_branch_b"].shape),
                  _resident(lp["w_branch_c"].shape), _resident(lp["w_out"].shape)],
        out_specs=tok(D_MODEL),
        out_shape=jax.ShapeDtypeStruct(h.shape, F32),
        compiler_params=_params(("parallel", "parallel")),
        name="merge",
    )(h, lp["norm_mix"], lp["w_gates"], ya, yb, yc, lp["w_branch_a"], lp["w_branch_b"],
      lp["w_branch_c"], lp["w_out"])


def _ffn_kernel(h_ref, g_ref, wgu_ref, wd_ref, gfin_ref, out_ref, act_scr, *, final):
    x = h_ref[...]
    xn = _rms(x, g_ref[...]).astype(BF16)
    for c0 in range(0, FFN_HIDDEN, FFN_CHUNK):
        gate = _dot(xn, wgu_ref[:, c0:c0 + FFN_CHUNK])
        up = _dot(xn, wgu_ref[:, FFN_HIDDEN + c0:FFN_HIDDEN + c0 + FFN_CHUNK])
        act_scr[:, c0:c0 + FFN_CHUNK] = (gate * _sigmoid(gate) * up).astype(BF16)
    y = x + _dot(act_scr[...], wd_ref[...])
    if final:
        y = _rms(y, gfin_ref[...])
    out_ref[...] = y


def _ffn(h2d, lp, g_final, final, tm):
    T = h2d.shape[0]
    tok = pl.BlockSpec((tm, D_MODEL), lambda i: (i, 0))
    return pl.pallas_call(
        functools.partial(_ffn_kernel, final=final),
        grid=(T // tm,),
        in_specs=[tok, _resident((1, D_MODEL)), _resident(lp["w_gate_up"].shape),
                  _resident(lp["w_down"].shape), _resident((1, D_MODEL))],
        out_specs=tok,
        out_shape=jax.ShapeDtypeStruct(h2d.shape, F32),
        scratch_shapes=[pltpu.VMEM((tm, FFN_HIDDEN), BF16)],
        compiler_params=_params(("parallel",)),
        name="ffn",
    )(h2d, lp["norm_ffn"], lp["w_gate_up"], lp["w_down"], g_final)


def _cmul(a, b):
    return a[0] * b[0] - a[1] * b[1], a[0] * b[1] + a[1] * b[0]


def _s5_tables(a_re, a_im, b_re, b_im, c_re, c_im, log_dt):
    dt = jnp.exp(log_dt)[:, None]
    mag = jnp.exp(a_re * dt)
    ab = (mag * jnp.cos(a_im * dt), mag * jnp.sin(a_im * dt))
    num_re, num_im = ab[0] - 1.0, ab[1]
    den = a_re * a_re + a_im * a_im
    f_re = (num_re * a_re + num_im * a_im) / den
    f_im = (num_im * a_re - num_re * a_im) / den
    bb_re = f_re[..., None] * b_re - f_im[..., None] * b_im
    bb_im = f_re[..., None] * b_im + f_im[..., None] * b_re
    eye = jnp.eye(SSM_GROUPS, dtype=F32)
    bd_in = lambda m: jnp.einsum("gpc,gh->gchp", m, eye).reshape(SSM_WIDTH, N_STATE)
    bmat = jnp.concatenate([bd_in(bb_re), bd_in(bb_im)], axis=1).astype(BF16)
    bd_out = lambda m: jnp.einsum("gcp,gh->gphc", m, eye).reshape(N_STATE, SSM_WIDTH)
    cmat = jnp.concatenate([bd_out(c_re), -bd_out(c_im)], axis=0).astype(BF16)
    a1 = (ab[0].reshape(1, N_STATE), ab[1].reshape(1, N_STATE))
    pows = [a1]
    for _ in range(SUBLANES - 1):
        pows.append(_cmul(pows[-1], a1))
    row = jnp.arange(SUBLANES)[:, None]
    tab = []
    for k in range(3):
        p = pows[(1 << k) - 1]
        keep = row >= (1 << k)
        tab += [jnp.where(keep, p[0], 0.0), jnp.where(keep, p[1], 0.0)]
    tab += [jnp.concatenate([p[0] for p in pows], axis=0),
            jnp.concatenate([p[1] for p in pows], axis=0)]
    return bmat, cmat, jnp.stack(tab)


def _layer_params(i, norm_mix, w_in, ssm_a_re, ssm_a_im, ssm_b_re, ssm_b_im, ssm_c_re, ssm_c_im,
                  ssm_d, ssm_log_dt, ssm_w_glu, sgu_norm, sgu_w, sgu_b,
                  w_branch_a, w_branch_b, w_branch_c, w_out, norm_ffn, w_gate_up, w_down):
    bmat, cmat, tab = _s5_tables(ssm_a_re[i], ssm_a_im[i], ssm_b_re[i], ssm_b_im[i],
                                 ssm_c_re[i], ssm_c_im[i], ssm_log_dt[i])
    w_in_b = w_in[i].astype(BF16)
    return {
        "norm_mix": norm_mix[i].reshape(1, D_MODEL),
        "w_mix": w_in_b[:, :KV_OFFSET],
        "w_kv": w_in_b[:, KV_OFFSET:GATE_OFFSET],
        "w_kv_t": w_in_b[:, KV_OFFSET:GATE_OFFSET].T,
        "w_gates": w_in_b[:, GATE_OFFSET:],
        "bmat": bmat, "cmat": cmat, "tab": tab,
        "d": ssm_d[i].reshape(1, SSM_WIDTH),
        "w_glu": ssm_w_glu[i].astype(BF16),
        "sgu_norm": sgu_norm[i].reshape(1, SGU_WIDTH),
        "sgu_w": sgu_w[i],
        "sgu_bias": jnp.repeat(sgu_b[i].T, SGU_GROUP_DIM, axis=1),
        "w_branch_a": w_branch_a[i].astype(BF16),
        "w_branch_b": w_branch_b[i].astype(BF16),
        "w_branch_c": w_branch_c[i].astype(BF16),
        "w_out": w_out[i].astype(BF16),
        "norm_ffn": norm_ffn[i].reshape(1, D_MODEL),
        "w_gate_up": w_gate_up[i].astype(BF16),
        "w_down": w_down[i].astype(BF16),
    }


def _layer(h, s0, k_stack, v_stack, k_cache, v_cache, lp, g_final, layer, depth):
    bsz, L, _ = h.shape
    prompt = k_cache is None
    if prompt:
        nb, tl = 1, min(L, TOKEN_TILE)
        tb, n = tl, min(L, SGU_CHUNK)
    else:
        nb, tl = bsz, L
        tb, n = L, L
    uav, q, k_stack, v_stack = _inproj(h, lp["norm_mix"], lp["w_mix"],
                                       lp["w_kv_t"] if prompt else lp["w_kv"], k_stack, v_stack,
                                       layer, depth, nb, tl, prompt)
    ya, yb, vn, sfin = _mixab(uav, s0, lp, tb, n)
    if prompt:
        tq = min(L, ATTN_BLOCK)
        yc = _attention(q, k_stack, v_stack, k_stack, v_stack, layer, tq, tq, 0, 1, True)
    else:
        past = k_cache.shape[4]
        tk = min(past, ATTN_BLOCK)
        yc = _attention(q, k_stack, v_stack, k_cache, v_cache, layer, L, tk, past // tk, 0, False)
    h1 = _merge(h, ya, yb, yc, lp, nb, tl)
    T = bsz * L
    h2 = _ffn(h1.reshape(T, D_MODEL), lp, g_final, layer == depth - 1, min(T, TOKEN_TILE))
    return h2.reshape(bsz, L, D_MODEL), sfin, k_stack, v_stack, vn


def kernel(x_prompt, x_sample, state_ssm_re, state_ssm_im, cache_sb_k, cache_sb_v, norm_mix, w_in, ssm_a_re, ssm_a_im, ssm_b_re, ssm_b_im, ssm_c_re, ssm_c_im, ssm_d, ssm_log_dt, ssm_w_glu, sgu_norm, sgu_w, sgu_b, w_branch_a, w_branch_b, w_branch_c, w_out, norm_ffn, w_gate_up, w_down, norm_final):
    weights = (norm_mix, w_in, ssm_a_re, ssm_a_im, ssm_b_re, ssm_b_im, ssm_c_re, ssm_c_im,
               ssm_d, ssm_log_dt, ssm_w_glu, sgu_norm, sgu_w, sgu_b,
               w_branch_a, w_branch_b, w_branch_c, w_out, norm_ffn, w_gate_up, w_down)
    depth = w_in.shape[0]
    bp, bs = x_prompt.shape[0], x_sample.shape[0]
    g_final = norm_final.reshape(1, D_MODEL)
    zero_state = jnp.zeros((bp, 1, 2 * N_STATE), F32)
    hp, hs = x_prompt, x_sample
    kp = vp = ks = vs = None
    cache_k_t = jnp.swapaxes(cache_sb_k, -1, -2)
    cache_v_t = jnp.swapaxes(cache_sb_v, -1, -2)
    outs = [[] for _ in range(5)]
    for i in range(depth):
        lp = _layer_params(i, *weights)
        hp, sfin_p, kp, vp, _ = _layer(hp, zero_state, kp, vp, None, None, lp, g_final, i, depth)
        s0 = jnp.concatenate([state_ssm_re[i].reshape(bs, 1, N_STATE),
                              state_ssm_im[i].reshape(bs, 1, N_STATE)], axis=-1)
        hs, sfin_s, ks, vs, vn_s = _layer(hs, s0, ks, vs, cache_k_t, cache_v_t, lp, g_final,
                                          i, depth)
        state = lambda s, part: s[:, 0, part * N_STATE:(part + 1) * N_STATE].reshape(
            -1, SSM_GROUPS, SSM_STATE)
        for lst, val in zip(outs, (state(sfin_p, 0), state(sfin_p, 1),
                                   state(sfin_s, 0), state(sfin_s, 1), vn_s)):
            lst.append(val)
    p_re, p_im, s_re, s_im, s_vb = (jnp.stack(lst) for lst in outs)
    return (hp, hs, p_re, p_im, jnp.swapaxes(kp, -1, -2), jnp.swapaxes(vp, -1, -2),
            s_re, s_im, ks, vs, s_vb)
```

```python
import functools
import math

import jax
import jax.numpy as jnp
from jax import lax
from jax.experimental import pallas as pl
from jax.experimental.pallas import tpu as pltpu

F32 = jnp.float32
BF16 = jnp.bfloat16

D_MODEL = 1024
SSM_WIDTH = 256
SSM_GROUP = 16
SSM_GROUPS = 16
SSM_STATE = 64
N_STATE = SSM_GROUPS * SSM_STATE
SGU_WIDTH = 256
SGU_CHUNK = 128
SGU_GROUPS = 4
SGU_GROUP_DIM = SGU_WIDTH // SGU_GROUPS
SB_HEADS = 8
SB_HEAD_DIM = 64
SB_WIDTH = SB_HEADS * SB_HEAD_DIM
FFN_HIDDEN = 2816
MIX_WIDTH = SSM_WIDTH + 2 * SGU_WIDTH
QKV_OFFSET = MIX_WIDTH
KV_OFFSET = MIX_WIDTH + SB_WIDTH
GATE_OFFSET = MIX_WIDTH + 3 * SB_WIDTH
EPS = 1e-6

SUBLANES = 8
VMEM_LIMIT = 56 * 1024 * 1024
QK_SCALE = SB_HEAD_DIM ** -0.5 * math.log2(math.e)
DEAD_BITS = 100.0 * math.log2(math.e)
MASKED_SCORE = -1e30
ATTN_BLOCK = 256
FFN_CHUNK = 256
TOKEN_TILE = 512


def _rms(x, g):
    return x * lax.rsqrt(jnp.mean(x * x, axis=-1, keepdims=True) + EPS) * g


def _dot(a, b):
    return jnp.dot(a, b, preferred_element_type=F32)


def _sigmoid(x):
    return 1.0 / (1.0 + jnp.exp(-x))


def _split_bf16(x):
    hi = x.astype(BF16)
    lo = (x - hi.astype(F32)).astype(BF16)
    return hi, lo


def _params(semantics):
    return pltpu.CompilerParams(dimension_semantics=semantics, vmem_limit_bytes=VMEM_LIMIT)


def _resident(shape):
    zeros = (0,) * len(shape)
    return pl.BlockSpec(shape, lambda *_: zeros, pipeline_mode=pl.Buffered(1))


def _inproj_kernel(*refs, nb, tl, transposed, fresh):
    if fresh:
        h_ref, g_ref, w_ref, wkv_ref, uav_ref, q_ref, k_ref, v_ref = refs
    else:
        h_ref, g_ref, w_ref, wkv_ref, _, _, uav_ref, q_ref, k_ref, v_ref = refs
    x = h_ref[...].reshape(nb * tl, D_MODEL)
    xn = _rms(x, g_ref[...]).astype(BF16)
    uav_ref[...] = _dot(xn, w_ref[:, 0:MIX_WIDTH]).reshape(nb, tl, MIX_WIDTH)
    q = _dot(xn, w_ref[:, QKV_OFFSET:QKV_OFFSET + SB_WIDTH]) * QK_SCALE
    q_ref[...] = q.astype(BF16).reshape(nb, tl, SB_WIDTH)
    dh = SB_HEAD_DIM
    for idx, ref in enumerate((k_ref, v_ref)):
        if transposed:
            seg = lax.dot_general(wkv_ref[idx * SB_WIDTH:(idx + 1) * SB_WIDTH, :], xn,
                                  (((1,), (1,)), ((), ())), preferred_element_type=F32)
        else:
            seg = _dot(xn, wkv_ref[:, idx * SB_WIDTH:(idx + 1) * SB_WIDTH])
        for bb in range(nb):
            for hh in range(SB_HEADS):
                if transposed:
                    ref[0, bb, hh] = seg[hh * dh:(hh + 1) * dh, bb * tl:(bb + 1) * tl]
                else:
                    ref[0, bb, hh] = seg[bb * tl:(bb + 1) * tl, hh * dh:(hh + 1) * dh]
        if fresh:
            ref[1:] = jnp.zeros((ref.shape[0] - 1,) + ref.shape[1:], F32)


def _inproj(h, g, w_mix, w_kv, k_stack, v_stack, layer, depth, nb, tl, transposed):
    bsz, L, _ = h.shape
    fresh = k_stack is None
    minor = (SB_HEAD_DIM, L) if transposed else (L, SB_HEAD_DIM)
    minor_blk = (SB_HEAD_DIM, tl) if transposed else (tl, SB_HEAD_DIM)
    stack_shape = jax.ShapeDtypeStruct((depth, bsz, SB_HEADS) + minor, F32)
    pos = (lambda i: (0, i)) if transposed else (lambda i: (i, 0))
    first = 0 if fresh else layer
    stack_spec = pl.BlockSpec((depth if fresh else 1, nb, SB_HEADS) + minor_blk,
                              lambda b, i: (first, b, 0) + pos(i))
    tok = lambda width: pl.BlockSpec((nb, tl, width), lambda b, i: (b, i, 0))
    in_specs = [tok(D_MODEL), _resident((1, D_MODEL)), _resident(w_mix.shape),
                _resident(w_kv.shape)]
    args = [h, g, w_mix, w_kv]
    aliases = {}
    if not fresh:
        in_specs += [pl.BlockSpec(memory_space=pl.ANY)] * 2
        args += [k_stack, v_stack]
        aliases = {4: 2, 5: 3}
    return pl.pallas_call(
        functools.partial(_inproj_kernel, nb=nb, tl=tl, transposed=transposed, fresh=fresh),
        grid=(bsz // nb, L // tl),
        in_specs=in_specs,
        out_specs=[tok(MIX_WIDTH), tok(SB_WIDTH), stack_spec, stack_spec],
        out_shape=[jax.ShapeDtypeStruct((bsz, L, MIX_WIDTH), F32),
                   jax.ShapeDtypeStruct((bsz, L, SB_WIDTH), BF16), stack_shape, stack_shape],
        input_output_aliases=aliases,
        compiler_params=_params(("parallel", "parallel")),
        name="inproj",
    )(*args)


def _gelu_tanh(y):
    c = math.sqrt(2.0 / math.pi)
    return 0.5 * y * (1.0 + jnp.tanh(c * (y + 0.044715 * (y * y * y))))


def _mixab_kernel(uav_ref, uavp_ref, s0_ref, bmat_ref, tab_ref, cmat_ref, d_ref, wglu_ref,
                  lng_ref, wsgu_ref, bsgu_ref,
                  ya_ref, yb_ref, vn_ref, sfin_ref,
                  x_scr, carry_scr, *, tb, n):
    i = pl.program_id(1)
    nblk = pl.num_programs(1) - 1
    half = N_STATE // 2

    @pl.when(i == 0)
    def _():
        carry_scr[...] = s0_ref[0]
        x_scr[...] = jnp.zeros((tb, 2 * N_STATE), F32)

    up = uavp_ref[0, :, 0:SSM_WIDTH]
    y = _dot(x_scr[...].astype(BF16), cmat_ref[...]) + d_ref[...] * up

    u = uav_ref[0, :, 0:SSM_WIDTH]
    x_scr[...] = _dot(u.astype(BF16), bmat_ref[...])

    def tile_scan(j, carry):
        rows = slice(j * SUBLANES, (j + 1) * SUBLANES)
        new_carry = []
        for hf in range(2):
            re_cols = slice(hf * half, (hf + 1) * half)
            im_cols = slice(N_STATE + hf * half, N_STATE + (hf + 1) * half)
            xr = x_scr[rows, re_cols]
            xi = x_scr[rows, im_cols]
            for k in range(3):
                sr = pltpu.roll(xr, 1 << k, 0)
                si = pltpu.roll(xi, 1 << k, 0)
                ar = tab_ref[2 * k, :, re_cols]
                ai = tab_ref[2 * k + 1, :, re_cols]
                xr, xi = xr + (ar * sr - ai * si), xi + (ar * si + ai * sr)
            cr, ci = carry[2 * hf], carry[2 * hf + 1]
            pr = tab_ref[6, :, re_cols]
            pi_ = tab_ref[7, :, re_cols]
            xr, xi = xr + (pr * cr - pi_ * ci), xi + (pr * ci + pi_ * cr)
            x_scr[rows, re_cols] = xr
            x_scr[rows, im_cols] = xi
            new_carry += [xr[SUBLANES - 1:SUBLANES], xi[SUBLANES - 1:SUBLANES]]
        return tuple(new_carry)

    def glu_piece(rows):
        def piece():
            z = _dot(_gelu_tanh(y[rows]).astype(BF16), wglu_ref[...])
            ya_ref[0, rows] = (z[:, 0:SSM_WIDTH] * _sigmoid(z[:, SSM_WIDTH:])).astype(ya_ref.dtype)
        return piece

    row = lax.broadcasted_iota(jnp.int32, (n, n), 0)
    col = lax.broadcasted_iota(jnp.int32, (n, n), 1)
    group = lax.broadcasted_iota(jnp.int32, (n, SGU_WIDTH), 1) // SGU_GROUP_DIM
    w_tril = [jnp.where(row >= col, wsgu_ref[g], 0.0).astype(BF16) for g in range(SGU_GROUPS)]

    def sgu_piece(rows):
        def piece():
            ub = uavp_ref[0, rows, SSM_WIDTH:SSM_WIDTH + SGU_WIDTH]
            vb = uavp_ref[0, rows, SSM_WIDTH + SGU_WIDTH:MIX_WIDTH]
            xc = vb - jnp.mean(vb, axis=-1, keepdims=True)
            vn = xc * lax.rsqrt(jnp.mean(xc * xc, axis=-1, keepdims=True) + EPS) * lng_ref[...]
            vn_ref[0, rows] = vn
            vnb = vn.astype(BF16)
            mixed = bsgu_ref[...]
            for g in range(SGU_GROUPS):
                mixed = mixed + jnp.where(group == g, _dot(w_tril[g], vnb), 0.0)
            yb_ref[0, rows] = (ub * mixed).astype(yb_ref.dtype)
        return piece

    glu_rows = min(tb, 2 * SGU_CHUNK)
    pieces = []
    for r0 in range(0, tb, glu_rows):
        pieces.append(glu_piece(slice(r0, r0 + glu_rows)))
        pieces += [sgu_piece(slice(c0, c0 + n)) for c0 in range(r0, r0 + glu_rows, n)]

    n_tiles = tb // SUBLANES
    every = max(n_tiles // len(pieces), 1)
    carry_in = (carry_scr[:, 0:half], carry_scr[:, N_STATE:N_STATE + half],
                carry_scr[:, half:N_STATE], carry_scr[:, N_STATE + half:2 * N_STATE])
    c = carry_in
    for j in range(n_tiles):
        c = tile_scan(j, c)
        if pieces and (j + 1) % every == 0:
            pieces.pop(0)()
    for piece in pieces:
        piece()

    keep = i < nblk
    for part, cols in enumerate((slice(0, half), slice(N_STATE, N_STATE + half),
                                 slice(half, N_STATE), slice(N_STATE + half, 2 * N_STATE))):
        carry_scr[:, cols] = jnp.where(keep, c[part], carry_in[part])
    sfin_ref[0] = carry_scr[...]


def _mixab(uav, s0, lp, tb, n):
    bsz, L, _ = uav.shape
    nblk = L // tb
    cur = lambda b, i: (b, jnp.minimum(i, nblk - 1), 0)
    prev = lambda b, i: (b, jnp.maximum(i - 1, 0), 0)
    tok = lambda width, dtype: (pl.BlockSpec((1, tb, width), prev),
                                jax.ShapeDtypeStruct((bsz, L, width), dtype))
    ya_spec, ya_shape = tok(SSM_WIDTH, BF16)
    yb_spec, yb_shape = tok(SGU_WIDTH, BF16)
    vn_spec, vn_shape = tok(SGU_WIDTH, F32)
    state_spec = pl.BlockSpec((1, 1, 2 * N_STATE), lambda b, i: (b, 0, 0))
    return pl.pallas_call(
        functools.partial(_mixab_kernel, tb=tb, n=n),
        grid=(bsz, nblk + 1),
        in_specs=[pl.BlockSpec((1, tb, MIX_WIDTH), cur),
                  pl.BlockSpec((1, tb, MIX_WIDTH), prev),
                  state_spec,
                  _resident(lp["bmat"].shape),
                  _resident(lp["tab"].shape), _resident(lp["cmat"].shape),
                  _resident(lp["d"].shape), _resident(lp["w_glu"].shape),
                  _resident(lp["sgu_norm"].shape), _resident((SGU_GROUPS, n, n)),
                  _resident((n, SGU_WIDTH))],
        out_specs=[ya_spec, yb_spec, vn_spec, state_spec],
        out_shape=[ya_shape, yb_shape, vn_shape,
                   jax.ShapeDtypeStruct((bsz, 1, 2 * N_STATE), F32)],
        scratch_shapes=[pltpu.VMEM((tb, 2 * N_STATE), F32), pltpu.VMEM((1, 2 * N_STATE), F32)],
        compiler_params=_params(("parallel", "arbitrary")),
        name="mixab",
    )(uav, uav, s0, lp["bmat"], lp["tab"], lp["cmat"], lp["d"], lp["w_glu"],
      lp["sgu_norm"], lp["sgu_w"][:, :n, :n], lp["sgu_bias"][:n])


def _attn_kernel(q_ref, kd_ref, vd_ref, kp_ref, vp_ref, kh_hbm, vh_hbm, o_ref, kbuf, vbuf, sem, *,
                 layer, tq, tk, hist0, hist_per_q, diag_transposed):
    b = pl.program_id(0)
    i = pl.program_id(1)
    n_hist = hist0 + i * hist_per_q
    heads = range(SB_HEADS)
    dh = SB_HEAD_DIM
    nt = (((1,), (1,)), ((), ()))
    qs = [q_ref[0, :, h * dh:(h + 1) * dh] for h in heads]

    def suffix_ones(size):
        r = lax.broadcasted_iota(jnp.int32, (size, size), 0)
        c = lax.broadcasted_iota(jnp.int32, (size, size), 1)
        return jnp.where(r >= c, 1.0, 0.0).astype(BF16)

    def decay(z2, ones):
        sp = jnp.maximum(z2, jnp.log2(1.0 + jnp.exp2(jnp.minimum(z2, 126.0))))
        hi, lo = _split_bf16(sp)
        return _dot(hi, ones) + _dot(lo, ones)

    def weights(z2, dec, dead):
        return jnp.exp2(z2 - (dec if dead is None else dec + dead)).astype(BF16)

    def history(k_t, v_t, ones, deads, accs, scale):
        zs = [_dot(qs[h], k_t[h]) for h in heads]
        decs = [decay(zs[h], ones) for h in heads]
        new_deads, new_accs = [], []
        for h in heads:
            pv = lax.dot_general(weights(zs[h], decs[h], deads[h]), v_t[h], nt,
                                 preferred_element_type=F32)
            total = decs[h][:, 0:1]
            if scale is not None:
                pv, total = scale * pv, scale * total
            new_accs.append(accs[h] + pv)
            new_deads.append(deads[h] + total)
        return tuple(new_deads), tuple(new_accs)

    r = lax.broadcasted_iota(jnp.int32, (tq, tq), 0)
    c = lax.broadcasted_iota(jnp.int32, (tq, tq), 1)
    ones_q = suffix_ones(tq)
    ones_k = ones_q if tk == tq else suffix_ones(tk)
    if diag_transposed:
        zd = [_dot(qs[h], kd_ref[0, 0, h].astype(BF16)) for h in heads]
    else:
        zd = [lax.dot_general(qs[h], kd_ref[0, 0, h].astype(BF16), nt, preferred_element_type=F32)
              for h in heads]
    zd = [jnp.where(c < r, z, MASKED_SCORE) for z in zd]
    dd = [decay(z, ones_q) for z in zd]
    accs = []
    for h in heads:
        w = weights(zd[h], dd[h], None)
        vd = vd_ref[0, 0, h].astype(BF16)
        accs.append(lax.dot_general(w, vd, nt, preferred_element_type=F32) if diag_transposed
                    else _dot(w, vd))
    deads = tuple(d[:, 0:1] for d in dd)

    has_prev = (n_hist > 0).astype(F32)
    deads, accs = history([kp_ref[0, 0, h].astype(BF16) for h in heads],
                          [vp_ref[0, 0, h].astype(BF16) for h in heads],
                          ones_k, deads, tuple(accs), has_prev)

    def fetch(j):
        cols = pl.ds(pl.multiple_of(j * tk, tk), tk)
        return (pltpu.make_async_copy(kh_hbm.at[layer, b, :, :, cols], kbuf, sem.at[0]),
                pltpu.make_async_copy(vh_hbm.at[layer, b, :, :, cols], vbuf, sem.at[1]))

    def more(carry):
        j, deads, _ = carry
        least = deads[0]
        for h in heads[1:]:
            least = jnp.minimum(least, deads[h])
        return jnp.logical_and(j >= 0, jnp.min(least) < DEAD_BITS)

    def step(carry):
        j, deads, accs = carry
        copies = fetch(j)
        for cp in copies:
            cp.start()
        for cp in copies:
            cp.wait()
        deads, accs = history([kbuf[h].astype(BF16) for h in heads],
                              [vbuf[h].astype(BF16) for h in heads], ones_k, deads, accs, None)
        return j - 1, deads, accs

    _, _, accs = lax.while_loop(more, step, (n_hist - 2, deads, accs))
    o_ref[0] = jnp.concatenate(accs, axis=1).astype(o_ref.dtype)


def _attention(q, k_new, v_new, k_hist, v_hist, layer, tq, tk, hist0, hist_per_q, diag_transposed):
    bsz, L, _ = q.shape
    H, Dh = SB_HEADS, SB_HEAD_DIM
    if diag_transposed:
        diag_spec = pl.BlockSpec((1, 1, H, Dh, tq), lambda b, i: (layer, b, 0, 0, i))
    else:
        diag_spec = pl.BlockSpec((1, 1, H, tq, Dh), lambda b, i: (layer, b, 0, i, 0))
    prev_spec = pl.BlockSpec(
        (1, 1, H, Dh, tk),
        lambda b, i: (layer, b, 0, 0, jnp.maximum(hist0 + i * hist_per_q - 1, 0)))
    any_spec = pl.BlockSpec(memory_space=pl.ANY)
    tok = pl.BlockSpec((1, tq, H * Dh), lambda b, i: (b, i, 0))
    return pl.pallas_call(
        functools.partial(_attn_kernel, layer=layer, tq=tq, tk=tk, hist0=hist0,
                          hist_per_q=hist_per_q, diag_transposed=diag_transposed),
        grid=(bsz, L // tq),
        in_specs=[tok, diag_spec, diag_spec, prev_spec, prev_spec, any_spec, any_spec],
        out_specs=tok,
        out_shape=jax.ShapeDtypeStruct((bsz, L, H * Dh), BF16),
        scratch_shapes=[pltpu.VMEM((H, Dh, tk), F32), pltpu.VMEM((H, Dh, tk), F32),
                        pltpu.SemaphoreType.DMA((2,))],
        compiler_params=_params(("parallel", "arbitrary")),
        name="stickbreak",
    )(q, k_new, v_new, k_hist, v_hist, k_hist, v_hist)


def _merge_kernel(h_ref, g_ref, wg_ref, ya_ref, yb_ref, yc_ref, wa_ref, wb_ref, wc_ref, wout_ref,
                  hout_ref, *, nb, tl):
    m = nb * tl
    x = h_ref[...].reshape(m, D_MODEL)
    xn = _rms(x, g_ref[...]).astype(BF16)

    def gate(idx):
        return _sigmoid(_dot(xn, wg_ref[:, idx * D_MODEL:(idx + 1) * D_MODEL]))

    merged = gate(0) * _dot(ya_ref[...].reshape(m, SSM_WIDTH), wa_ref[...])
    merged = merged + gate(1) * _dot(yb_ref[...].reshape(m, SGU_WIDTH), wb_ref[...])
    merged = merged + gate(2) * _dot(yc_ref[...].reshape(m, SB_WIDTH), wc_ref[...])
    hout_ref[...] = (x + _dot(merged.astype(BF16), wout_ref[...])).reshape(nb, tl, D_MODEL)


def _merge(h, ya, yb, yc, lp, nb, tl):
    bsz, L, _ = h.shape
    tok = lambda width: pl.BlockSpec((nb, tl, width), lambda b, i: (b, i, 0))
    return pl.pallas_call(
        functools.partial(_merge_kernel, nb=nb, tl=tl),
        grid=(bsz // nb, L // tl),
        in_specs=[tok(D_MODEL), _resident((1, D_MODEL)), _resident(lp["w_gates"].shape),
                  tok(SSM_WIDTH), tok(SGU_WIDTH), tok(SB_WIDTH),
                  _resident(lp["w_branch_a"].shape), _resident(lp["w_branch_b"].shape),
                  _resident(lp["w_branch_c"].shape), _resident(lp["w_out"].shape)],
        out_specs=tok(D_MODEL),
        out_shape=jax.ShapeDtypeStruct(h.shape, F32),
        compiler_params=_params(("parallel", "parallel")),
        name="merge",
    )(h, lp["norm_mix"], lp["w_gates"], ya, yb, yc, lp["w_branch_a"], lp["w_branch_b"],
      lp["w_branch_c"], lp["w_out"])


def _ffn_kernel(h_ref, g_ref, wgu_ref, wd_ref, gfin_ref, out_ref, act_scr, *, final):
    x = h_ref[...]
    xn = _rms(x, g_ref[...]).astype(BF16)
    for c0 in range(0, FFN_HIDDEN, FFN_CHUNK):
        gate = _dot(xn, wgu_ref[:, c0:c0 + FFN_CHUNK])
        up = _dot(xn, wgu_ref[:, FFN_HIDDEN + c0:FFN_HIDDEN + c0 + FFN_CHUNK])
        act_scr[:, c0:c0 + FFN_CHUNK] = (gate * _sigmoid(gate) * up).astype(BF16)
    y = x + _dot(act_scr[...], wd_ref[...])
    if final:
        y = _rms(y, gfin_ref[...])
    out_ref[...] = y


def _ffn(h2d, lp, g_final, final, tm):
    T = h2d.shape[0]
    tok = pl.BlockSpec((tm, D_MODEL), lambda i: (i, 0))
    return pl.pallas_call(
        functools.partial(_ffn_kernel, final=final),
        grid=(T // tm,),
        in_specs=[tok, _resident((1, D_MODEL)), _resident(lp["w_gate_up"].shape),
                  _resident(lp["w_down"].shape), _resident((1, D_MODEL))],
        out_specs=tok,
        out_shape=jax.ShapeDtypeStruct(h2d.shape, F32),
        scratch_shapes=[pltpu.VMEM((tm, FFN_HIDDEN), BF16)],
        compiler_params=_params(("parallel",)),
        name="ffn",
    )(h2d, lp["norm_ffn"], lp["w_gate_up"], lp["w_down"], g_final)


def _cmul(a, b):
    return a[0] * b[0] - a[1] * b[1], a[0] * b[1] + a[1] * b[0]


def _s5_tables(a_re, a_im, b_re, b_im, c_re, c_im, log_dt):
    dt = jnp.exp(log_dt)[:, None]
    mag = jnp.exp(a_re * dt)
    ab = (mag * jnp.cos(a_im * dt), mag * jnp.sin(a_im * dt))
    num_re, num_im = ab[0] - 1.0, ab[1]
    den = a_re * a_re + a_im * a_im
    f_re = (num_re * a_re + num_im * a_im) / den
    f_im = (num_im * a_re - num_re * a_im) / den
    bb_re = f_re[..., None] * b_re - f_im[..., None] * b_im
    bb_im = f_re[..., None] * b_im + f_im[..., None] * b_re
    eye = jnp.eye(SSM_GROUPS, dtype=F32)
    bd_in = lambda m: jnp.einsum("gpc,gh->gchp", m, eye).reshape(SSM_WIDTH, N_STATE)
    bmat = jnp.concatenate([bd_in(bb_re), bd_in(bb_im)], axis=1).astype(BF16)
    bd_out = lambda m: jnp.einsum("gcp,gh->gphc", m, eye).reshape(N_STATE, SSM_WIDTH)
    cmat = jnp.concatenate([bd_out(c_re), -bd_out(c_im)], axis=0).astype(BF16)
    a1 = (ab[0].reshape(1, N_STATE), ab[1].reshape(1, N_STATE))
    pows = [a1]
    for _ in range(SUBLANES - 1):
        pows.append(_cmul(pows[-1], a1))
    row = jnp.arange(SUBLANES)[:, None]
    tab = []
    for k in range(3):
        p = pows[(1 << k) - 1]
        keep = row >= (1 << k)
        tab += [jnp.where(keep, p[0], 0.0), jnp.where(keep, p[1], 0.0)]
    tab += [jnp.concatenate([p[0] for p in pows], axis=0),
            jnp.concatenate([p[1] for p in pows], axis=0)]
    return bmat, cmat, jnp.stack(tab)


def _layer_params(i, norm_mix, w_in, ssm_a_re, ssm_a_im, ssm_b_re, ssm_b_im, ssm_c_re, ssm_c_im,
                  ssm_d, ssm_log_dt, ssm_w_glu, sgu_norm, sgu_w, sgu_b,
                  w_branch_a, w_branch_b, w_branch_c, w_out, norm_ffn, w_gate_up, w_down):
    bmat, cmat, tab = _s5_tables(ssm_a_re[i], ssm_a_im[i], ssm_b_re[i], ssm_b_im[i],
                                 ssm_c_re[i], ssm_c_im[i], ssm_log_dt[i])
    w_in_b = w_in[i].astype(BF16)
    return {
        "norm_mix": norm_mix[i].reshape(1, D_MODEL),
        "w_mix": w_in_b[:, :KV_OFFSET],
        "w_kv": w_in_b[:, KV_OFFSET:GATE_OFFSET],
        "w_kv_t": w_in_b[:, KV_OFFSET:GATE_OFFSET].T,
        "w_gates": w_in_b[:, GATE_OFFSET:],
        "bmat": bmat, "cmat": cmat, "tab": tab,
        "d": ssm_d[i].reshape(1, SSM_WIDTH),
        "w_glu": ssm_w_glu[i].astype(BF16),
        "sgu_norm": sgu_norm[i].reshape(1, SGU_WIDTH),
        "sgu_w": sgu_w[i],
        "sgu_bias": jnp.repeat(sgu_b[i].T, SGU_GROUP_DIM, axis=1),
        "w_branch_a": w_branch_a[i].astype(BF16),
        "w_branch_b": w_branch_b[i].astype(BF16),
        "w_branch_c": w_branch_c[i].astype(BF16),
        "w_out": w_out[i].astype(BF16),
        "norm_ffn": norm_ffn[i].reshape(1, D_MODEL),
        "w_gate_up": w_gate_up[i].astype(BF16),
        "w_down": w_down[i].astype(BF16),
    }


def _layer(h, s0, k_stack, v_stack, k_cache, v_cache, lp, g_final, layer, depth):
    bsz, L, _ = h.shape
    prompt = k_cache is None
    if prompt:
        nb, tl = 1, min(L, TOKEN_TILE)
        tb, n = tl, min(L, SGU_CHUNK)
    else:
        nb, tl = bsz, L
        tb, n = L, L
    uav, q, k_stack, v_stack = _inproj(h, lp["norm_mix"], lp["w_mix"],
                                       lp["w_kv_t"] if prompt else lp["w_kv"], k_stack, v_stack,
                                       layer, depth, nb, tl, prompt)
    ya, yb, vn, sfin = _mixab(uav, s0, lp, tb, n)
    if prompt:
        tq = min(L, ATTN_BLOCK)
        yc = _attention(q, k_stack, v_stack, k_stack, v_stack, layer, tq, tq, 0, 1, True)
    else:
        past = k_cache.shape[4]
        tk = min(past, ATTN_BLOCK)
        yc = _attention(q, k_stack, v_stack, k_cache, v_cache, layer, L, tk, past // tk, 0, False)
    h1 = _merge(h, ya, yb, yc, lp, nb, tl)
    T = bsz * L
    h2 = _ffn(h1.reshape(T, D_MODEL), lp, g_final, layer == depth - 1, min(T, TOKEN_TILE))
    return h2.reshape(bsz, L, D_MODEL), sfin, k_stack, v_stack, vn


def kernel(x_prompt, x_sample, state_ssm_re, state_ssm_im, cache_sb_k, cache_sb_v, norm_mix, w_in, ssm_a_re, ssm_a_im, ssm_b_re, ssm_b_im, ssm_c_re, ssm_c_im, ssm_d, ssm_log_dt, ssm_w_glu, sgu_norm, sgu_w, sgu_b, w_branch_a, w_branch_b, w_branch_c, w_out, norm_ffn, w_gate_up, w_down, norm_final):
    weights = (norm_mix, w_in, ssm_a_re, ssm_a_im, ssm_b_re, ssm_b_im, ssm_c_re, ssm_c_im,
               ssm_d, ssm_log_dt, ssm_w_glu, sgu_norm, sgu_w, sgu_b,
               w_branch_a, w_branch_b, w_branch_c, w_out, norm_ffn, w_gate_up, w_down)
    depth = w_in.shape[0]
    bp, bs = x_prompt.shape[0], x_sample.shape[0]
    g_final = norm_final.reshape(1, D_MODEL)
    zero_state = jnp.zeros((bp, 1, 2 * N_STATE), F32)
    hp, hs = x_prompt, x_sample
    kp = vp = ks = vs = None
    cache_k_t = jnp.swapaxes(cache_sb_k, -1, -2)
    cache_v_t = jnp.swapaxes(cache_sb_v, -1, -2)
    outs = [[] for _ in range(5)]
    for i in range(depth):
        lp = _layer_params(i, *weights)
        hp, sfin_p, kp, vp, _ = _layer(hp, zero_state, kp, vp, None, None, lp, g_final, i, depth)
        s0 = jnp.concatenate([state_ssm_re[i].reshape(bs, 1, N_STATE),
                              state_ssm_im[i].reshape(bs, 1, N_STATE)], axis=-1)
        hs, sfin_s, ks, vs, vn_s = _layer(hs, s0, ks, vs, cache_k_t, cache_v_t, lp, g_final,
                                          i, depth)
        state = lambda s, part: s[:, 0, part * N_STATE:(part + 1) * N_STATE].reshape(
            -1, SSM_GROUPS, SSM_STATE)
        for lst, val in zip(outs, (state(sfin_p, 0), state(sfin_p, 1),
                                   state(sfin_s, 0), state(sfin_s, 1), vn_s)):
            lst.append(val)
    p_re, p_im, s_re, s_im, s_vb = (jnp.stack(lst) for lst in outs)
    return (hp, hs, p_re, p_im, jnp.swapaxes(kp, -1, -2), jnp.swapaxes(vp, -1, -2),
            s_re, s_im, ks, vs, s_vb)
```

```python
import functools
import math

import jax
import jax.numpy as jnp
from jax import lax
from jax.experimental import pallas as pl
from jax.experimental.pallas import tpu as pltpu

F32 = jnp.float32
BF16 = jnp.bfloat16

D_MODEL = 1024
SSM_WIDTH = 256
SSM_GROUP = 16
SSM_GROUPS = 16
SSM_STATE = 64
N_STATE = SSM_GROUPS * SSM_STATE
SGU_WIDTH = 256
SGU_CHUNK = 128
SGU_GROUPS = 4
SGU_GROUP_DIM = SGU_WIDTH // SGU_GROUPS
SB_HEADS = 8
SB_HEAD_DIM = 64
SB_WIDTH = SB_HEADS * SB_HEAD_DIM
FFN_HIDDEN = 2816
MIX_WIDTH = SSM_WIDTH + 2 * SGU_WIDTH
QKV_OFFSET = MIX_WIDTH
KV_OFFSET = MIX_WIDTH + SB_WIDTH
GATE_OFFSET = MIX_WIDTH + 3 * SB_WIDTH
EPS = 1e-6

SUBLANES = 8
VMEM_LIMIT = 56 * 1024 * 1024
QK_SCALE = SB_HEAD_DIM ** -0.5 * math.log2(math.e)
DEAD_BITS = 100.0 * math.log2(math.e)
MASKED_SCORE = -1e30
ATTN_BLOCK = 256
FFN_CHUNK = 256
TOKEN_TILE = 512


def _rms(x, g):
    return x * lax.rsqrt(jnp.mean(x * x, axis=-1, keepdims=True) + EPS) * g


def _dot(a, b):
    return jnp.dot(a, b, preferred_element_type=F32)


def _sigmoid(x):
    return 1.0 / (1.0 + jnp.exp(-x))


def _split_bf16(x):
    hi = x.astype(BF16)
    lo = (x - hi.astype(F32)).astype(BF16)
    return hi, lo


def _params(semantics):
    return pltpu.CompilerParams(dimension_semantics=semantics, vmem_limit_bytes=VMEM_LIMIT)


def _resident(shape):
    zeros = (0,) * len(shape)
    return pl.BlockSpec(shape, lambda *_: zeros, pipeline_mode=pl.Buffered(1))


def _layer_block(arr, layer):
    zeros = (0,) * (arr.ndim - 1)
    return pl.BlockSpec((None,) + arr.shape[1:], lambda *_: (layer,) + zeros,
                        pipeline_mode=pl.Buffered(1))


def _inproj_kernel(*refs, nb, tl, transposed, fresh):
    if fresh:
        h_ref, g_ref, w_ref, wkv_ref, uav_ref, q_ref, k_ref, v_ref = refs
    else:
        h_ref, g_ref, w_ref, wkv_ref, _, _, uav_ref, q_ref, k_ref, v_ref = refs
    x = h_ref[...].reshape(nb * tl, D_MODEL)
    xn = _rms(x, g_ref[...]).astype(BF16)
    uav_ref[...] = _dot(xn, w_ref[:, 0:MIX_WIDTH]).reshape(nb, tl, MIX_WIDTH)
    q = _dot(xn, w_ref[:, QKV_OFFSET:QKV_OFFSET + SB_WIDTH]) * QK_SCALE
    q_ref[...] = q.astype(BF16).reshape(nb, tl, SB_WIDTH)
    dh = SB_HEAD_DIM
    for idx, ref in enumerate((k_ref, v_ref)):
        if transposed:
            seg = lax.dot_general(wkv_ref[idx * SB_WIDTH:(idx + 1) * SB_WIDTH, :], xn,
                                  (((1,), (1,)), ((), ())), preferred_element_type=F32)
        else:
            seg = _dot(xn, wkv_ref[:, idx * SB_WIDTH:(idx + 1) * SB_WIDTH])
        for bb in range(nb):
            for hh in range(SB_HEADS):
                if transposed:
                    ref[0, bb, hh] = seg[hh * dh:(hh + 1) * dh, bb * tl:(bb + 1) * tl]
                else:
                    ref[0, bb, hh] = seg[bb * tl:(bb + 1) * tl, hh * dh:(hh + 1) * dh]
        if fresh:
            ref[1:] = jnp.zeros((ref.shape[0] - 1,) + ref.shape[1:], F32)


def _inproj(h, sp, k_stack, v_stack, layer, depth, nb, tl, transposed):
    bsz, L, _ = h.shape
    fresh = k_stack is None
    w_kv = sp["w_kv_t"] if transposed else sp["w_kv"]
    minor = (SB_HEAD_DIM, L) if transposed else (L, SB_HEAD_DIM)
    minor_blk = (SB_HEAD_DIM, tl) if transposed else (tl, SB_HEAD_DIM)
    stack_shape = jax.ShapeDtypeStruct((depth, bsz, SB_HEADS) + minor, F32)
    pos = (lambda i: (0, i)) if transposed else (lambda i: (i, 0))
    first = 0 if fresh else layer
    stack_spec = pl.BlockSpec((depth if fresh else 1, nb, SB_HEADS) + minor_blk,
                              lambda b, i: (first, b, 0) + pos(i))
    tok = lambda width: pl.BlockSpec((nb, tl, width), lambda b, i: (b, i, 0))
    in_specs = [tok(D_MODEL), _layer_block(sp["norm_mix"], layer),
                _layer_block(sp["w_mix"], layer), _layer_block(w_kv, layer)]
    args = [h, sp["norm_mix"], sp["w_mix"], w_kv]
    aliases = {}
    if not fresh:
        in_specs += [pl.BlockSpec(memory_space=pl.ANY)] * 2
        args += [k_stack, v_stack]
        aliases = {4: 2, 5: 3}
    return pl.pallas_call(
        functools.partial(_inproj_kernel, nb=nb, tl=tl, transposed=transposed, fresh=fresh),
        grid=(bsz // nb, L // tl),
        in_specs=in_specs,
        out_specs=[tok(MIX_WIDTH), tok(SB_WIDTH), stack_spec, stack_spec],
        out_shape=[jax.ShapeDtypeStruct((bsz, L, MIX_WIDTH), F32),
                   jax.ShapeDtypeStruct((bsz, L, SB_WIDTH), BF16), stack_shape, stack_shape],
        input_output_aliases=aliases,
        compiler_params=_params(("parallel", "parallel")),
        name="inproj",
    )(*args)


def _gelu_tanh(y):
    c = math.sqrt(2.0 / math.pi)
    return 0.5 * y * (1.0 + jnp.tanh(c * (y + 0.044715 * (y * y * y))))


def _mixab_kernel(uav_ref, uavp_ref, s0_ref, bmat_ref, tab_ref, cmat_ref, d_ref, wglu_ref,
                  lng_ref, wsgu_ref, bsgu_ref,
                  ya_ref, yb_ref, vn_ref, sfin_ref,
                  x_scr, carry_scr, *, tb, n):
    i = pl.program_id(1)
    nblk = pl.num_programs(1) - 1
    half = N_STATE // 2

    @pl.when(i == 0)
    def _():
        carry_scr[...] = s0_ref[0]
        x_scr[...] = jnp.zeros((tb, 2 * N_STATE), F32)

    up = uavp_ref[0, :, 0:SSM_WIDTH]
    u = uav_ref[0, :, 0:SSM_WIDTH]
    tile_row = lax.broadcasted_iota(jnp.int32, (tb, SSM_WIDTH), 0) % SUBLANES
    u_before = jnp.where(tile_row == 0, 0.0, pltpu.roll(u, 1, 0))
    u_pair = jnp.concatenate([u, u_before], axis=1).astype(BF16)
    y_parts = []
    for rows in (slice(0, tb // 2), slice(tb // 2, tb)):
        y_parts.append(_dot(x_scr[rows, :].astype(BF16), cmat_ref[...]) + d_ref[...] * up[rows])
        x_scr[rows, :] = _dot(u_pair[rows], bmat_ref[...])
    y = jnp.concatenate(y_parts, axis=0)

    def tile_scan(j, carry):
        rows = slice(j * SUBLANES, (j + 1) * SUBLANES)
        new_carry = []
        for hf in range(2):
            re_cols = slice(hf * half, (hf + 1) * half)
            im_cols = slice(N_STATE + hf * half, N_STATE + (hf + 1) * half)
            xr = x_scr[rows, re_cols]
            xi = x_scr[rows, im_cols]
            for k in (1, 2):
                sr = pltpu.roll(xr, 1 << k, 0)
                si = pltpu.roll(xi, 1 << k, 0)
                ar = tab_ref[2 * k - 2, :, re_cols]
                ai = tab_ref[2 * k - 1, :, re_cols]
                xr, xi = xr + (ar * sr - ai * si), xi + (ar * si + ai * sr)
            cr, ci = carry[2 * hf], carry[2 * hf + 1]
            pr = tab_ref[4, :, re_cols]
            pi_ = tab_ref[5, :, re_cols]
            xr, xi = xr + (pr * cr - pi_ * ci), xi + (pr * ci + pi_ * cr)
            x_scr[rows, re_cols] = xr
            x_scr[rows, im_cols] = xi
            new_carry += [xr[SUBLANES - 1:SUBLANES], xi[SUBLANES - 1:SUBLANES]]
        return tuple(new_carry)

    def glu_piece(rows):
        def piece():
            z = _dot(_gelu_tanh(y[rows]).astype(BF16), wglu_ref[...])
            ya_ref[0, rows] = (z[:, 0:SSM_WIDTH] * _sigmoid(z[:, SSM_WIDTH:])).astype(ya_ref.dtype)
        return piece

    row = lax.broadcasted_iota(jnp.int32, (n, n), 0)
    col = lax.broadcasted_iota(jnp.int32, (n, n), 1)
    group = lax.broadcasted_iota(jnp.int32, (n, SGU_WIDTH), 1) // SGU_GROUP_DIM
    w_tril = [jnp.where(row >= col, wsgu_ref[g], 0.0).astype(BF16) for g in range(SGU_GROUPS)]

    def sgu_piece(rows):
        def piece():
            ub = uavp_ref[0, rows, SSM_WIDTH:SSM_WIDTH + SGU_WIDTH]
            vb = uavp_ref[0, rows, SSM_WIDTH + SGU_WIDTH:MIX_WIDTH]
            xc = vb - jnp.mean(vb, axis=-1, keepdims=True)
            vn = xc * lax.rsqrt(jnp.mean(xc * xc, axis=-1, keepdims=True) + EPS) * lng_ref[...]
            vn_ref[0, rows] = vn
            vnb = vn.astype(BF16)
            mixed = bsgu_ref[...]
            for g in range(SGU_GROUPS):
                mixed = mixed + jnp.where(group == g, _dot(w_tril[g], vnb), 0.0)
            yb_ref[0, rows] = (ub * mixed).astype(yb_ref.dtype)
        return piece

    glu_rows = min(tb, 2 * SGU_CHUNK)
    pieces = []
    for r0 in range(0, tb, glu_rows):
        pieces.append(glu_piece(slice(r0, r0 + glu_rows)))
        pieces += [sgu_piece(slice(c0, c0 + n)) for c0 in range(r0, r0 + glu_rows, n)]

    n_tiles = tb // SUBLANES
    every = max(n_tiles // len(pieces), 1)
    carry_in = (carry_scr[:, 0:half], carry_scr[:, N_STATE:N_STATE + half],
                carry_scr[:, half:N_STATE], carry_scr[:, N_STATE + half:2 * N_STATE])
    c = carry_in
    for j in range(n_tiles):
        c = tile_scan(j, c)
        if pieces and (j + 1) % every == 0:
            pieces.pop(0)()
    for piece in pieces:
        piece()

    keep = i < nblk
    for part, cols in enumerate((slice(0, half), slice(N_STATE, N_STATE + half),
                                 slice(half, N_STATE), slice(N_STATE + half, 2 * N_STATE))):
        carry_scr[:, cols] = jnp.where(keep, c[part], carry_in[part])
    sfin_ref[0] = carry_scr[...]


def _mixab(uav, s0, sp, layer, tb, n):
    bsz, L, _ = uav.shape
    sgu_w, sgu_bias = sp["sgu"](n)
    params = [sp["bmat"], sp["tab"], sp["cmat"], sp["d"], sp["w_glu"], sp["sgu_norm"],
              sgu_w, sgu_bias]
    nblk = L // tb
    cur = lambda b, i: (b, jnp.minimum(i, nblk - 1), 0)
    prev = lambda b, i: (b, jnp.maximum(i - 1, 0), 0)
    tok = lambda width, dtype: (pl.BlockSpec((1, tb, width), prev),
                                jax.ShapeDtypeStruct((bsz, L, width), dtype))
    ya_spec, ya_shape = tok(SSM_WIDTH, BF16)
    yb_spec, yb_shape = tok(SGU_WIDTH, BF16)
    vn_spec, vn_shape = tok(SGU_WIDTH, F32)
    state_spec = pl.BlockSpec((1, 1, 2 * N_STATE), lambda b, i: (b, 0, 0))
    return pl.pallas_call(
        functools.partial(_mixab_kernel, tb=tb, n=n),
        grid=(bsz, nblk + 1),
        in_specs=[pl.BlockSpec((1, tb, MIX_WIDTH), cur),
                  pl.BlockSpec((1, tb, MIX_WIDTH), prev),
                  state_spec] + [_layer_block(p, layer) for p in params],
        out_specs=[ya_spec, yb_spec, vn_spec, state_spec],
        out_shape=[ya_shape, yb_shape, vn_shape,
                   jax.ShapeDtypeStruct((bsz, 1, 2 * N_STATE), F32)],
        scratch_shapes=[pltpu.VMEM((tb, 2 * N_STATE), F32), pltpu.VMEM((1, 2 * N_STATE), F32)],
        compiler_params=_params(("parallel", "arbitrary")),
        name="mixab",
    )(uav, uav, s0, *params)


def _attn_kernel(q_ref, kd_ref, vd_ref, kp_ref, vp_ref, kh_hbm, vh_hbm, o_ref, kbuf, vbuf, sem, *,
                 layer, tq, tk, hist0, hist_per_q, diag_transposed):
    b = pl.program_id(0)
    i = pl.program_id(1)
    n_hist = hist0 + i * hist_per_q
    heads = range(SB_HEADS)
    dh = SB_HEAD_DIM
    nt = (((1,), (1,)), ((), ()))
    qs = [q_ref[0, :, h * dh:(h + 1) * dh] for h in heads]

    def suffix_ones(size):
        r = lax.broadcasted_iota(jnp.int32, (size, size), 0)
        c = lax.broadcasted_iota(jnp.int32, (size, size), 1)
        return jnp.where(r >= c, 1.0, 0.0).astype(BF16)

    def decay(zs, ones):
        parts = []
        for z2 in zs:
            sp = jnp.maximum(z2, jnp.log2(1.0 + jnp.exp2(jnp.minimum(z2, 126.0))))
            parts += _split_bf16(sp)
        sums = _dot(jnp.concatenate(parts, axis=0), ones)
        return [sums[(2 * h) * tq:(2 * h + 1) * tq] + sums[(2 * h + 1) * tq:(2 * h + 2) * tq]
                for h in range(len(zs))]

    def weights(z2, dec, dead):
        return jnp.exp2(z2 - (dec if dead is None else dec + dead)).astype(BF16)

    def history(k_t, v_t, ones, deads, accs, scale):
        zs = [_dot(qs[h], k_t[h]) for h in heads]
        decs = decay(zs, ones)
        new_deads, new_accs = [], []
        for h in heads:
            pv = lax.dot_general(weights(zs[h], decs[h], deads[h]), v_t[h], nt,
                                 preferred_element_type=F32)
            total = decs[h][:, 0:1]
            if scale is not None:
                pv, total = scale * pv, scale * total
            new_accs.append(accs[h] + pv)
            new_deads.append(deads[h] + total)
        return tuple(new_deads), tuple(new_accs)

    r = lax.broadcasted_iota(jnp.int32, (tq, tq), 0)
    c = lax.broadcasted_iota(jnp.int32, (tq, tq), 1)
    ones_q = suffix_ones(tq)
    ones_k = ones_q if tk == tq else suffix_ones(tk)
    if diag_transposed:
        zd = [_dot(qs[h], kd_ref[0, 0, h].astype(BF16)) for h in heads]
    else:
        zd = [lax.dot_general(qs[h], kd_ref[0, 0, h].astype(BF16), nt, preferred_element_type=F32)
              for h in heads]
    zd = [jnp.where(c < r, z, MASKED_SCORE) for z in zd]
    dd = decay(zd, ones_q)
    accs = []
    for h in heads:
        w = weights(zd[h], dd[h], None)
        vd = vd_ref[0, 0, h].astype(BF16)
        accs.append(lax.dot_general(w, vd, nt, preferred_element_type=F32) if diag_transposed
                    else _dot(w, vd))
    deads = tuple(d[:, 0:1] for d in dd)

    has_prev = (n_hist > 0).astype(F32)
    deads, accs = history([kp_ref[0, 0, h].astype(BF16) for h in heads],
                          [vp_ref[0, 0, h].astype(BF16) for h in heads],
                          ones_k, deads, tuple(accs), has_prev)

    def fetch(j):
        cols = pl.ds(pl.multiple_of(j * tk, tk), tk)
        return (pltpu.make_async_copy(kh_hbm.at[layer, b, :, :, cols], kbuf, sem.at[0]),
                pltpu.make_async_copy(vh_hbm.at[layer, b, :, :, cols], vbuf, sem.at[1]))

    def more(carry):
        j, deads, _ = carry
        least = deads[0]
        for h in heads[1:]:
            least = jnp.minimum(least, deads[h])
        return jnp.logical_and(j >= 0, jnp.min(least) < DEAD_BITS)

    def step(carry):
        j, deads, accs = carry
        copies = fetch(j)
        for cp in copies:
            cp.start()
        for cp in copies:
            cp.wait()
        deads, accs = history([kbuf[h].astype(BF16) for h in heads],
                              [vbuf[h].astype(BF16) for h in heads], ones_k, deads, accs, None)
        return j - 1, deads, accs

    _, _, accs = lax.while_loop(more, step, (n_hist - 2, deads, accs))
    o_ref[0] = jnp.concatenate(accs, axis=1).astype(o_ref.dtype)


def _attention(q, k_new, v_new, k_hist, v_hist, layer, tq, tk, hist0, hist_per_q, diag_transposed):
    bsz, L, _ = q.shape
    H, Dh = SB_HEADS, SB_HEAD_DIM
    if diag_transposed:
        diag_spec = pl.BlockSpec((1, 1, H, Dh, tq), lambda b, i: (layer, b, 0, 0, i))
    else:
        diag_spec = pl.BlockSpec((1, 1, H, tq, Dh), lambda b, i: (layer, b, 0, i, 0))
    prev_spec = pl.BlockSpec(
        (1, 1, H, Dh, tk),
        lambda b, i: (layer, b, 0, 0, jnp.maximum(hist0 + i * hist_per_q - 1, 0)))
    any_spec = pl.BlockSpec(memory_space=pl.ANY)
    tok = pl.BlockSpec((1, tq, H * Dh), lambda b, i: (b, i, 0))
    return pl.pallas_call(
        functools.partial(_attn_kernel, layer=layer, tq=tq, tk=tk, hist0=hist0,
                          hist_per_q=hist_per_q, diag_transposed=diag_transposed),
        grid=(bsz, L // tq),
        in_specs=[tok, diag_spec, diag_spec, prev_spec, prev_spec, any_spec, any_spec],
        out_specs=tok,
        out_shape=jax.ShapeDtypeStruct((bsz, L, H * Dh), BF16),
        scratch_shapes=[pltpu.VMEM((H, Dh, tk), F32), pltpu.VMEM((H, Dh, tk), F32),
                        pltpu.SemaphoreType.DMA((2,))],
        compiler_params=_params(("parallel", "arbitrary")),
        name="stickbreak",
    )(q, k_new, v_new, k_hist, v_hist, k_hist, v_hist)


def _merge_kernel(h_ref, g_ref, wg_ref, ya_ref, yb_ref, yc_ref, wa_ref, wb_ref, wc_ref, wout_ref,
                  hout_ref, *, nb, tl):
    m = nb * tl
    x = h_ref[...].reshape(m, D_MODEL)
    xn = _rms(x, g_ref[...]).astype(BF16)

    def gate(idx):
        return _sigmoid(_dot(xn, wg_ref[:, idx * D_MODEL:(idx + 1) * D_MODEL]))

    merged = gate(0) * _dot(ya_ref[...].reshape(m, SSM_WIDTH), wa_ref[...])
    merged = merged + gate(1) * _dot(yb_ref[...].reshape(m, SGU_WIDTH), wb_ref[...])
    merged = merged + gate(2) * _dot(yc_ref[...].reshape(m, SB_WIDTH), wc_ref[...])
    hout_ref[...] = (x + _dot(merged.astype(BF16), wout_ref[...])).reshape(nb, tl, D_MODEL)


def _merge(h, ya, yb, yc, sp, layer, nb, tl):
    bsz, L, _ = h.shape
    tok = lambda width: pl.BlockSpec((nb, tl, width), lambda b, i: (b, i, 0))
    par = lambda name: _layer_block(sp[name], layer)
    return pl.pallas_call(
        functools.partial(_merge_kernel, nb=nb, tl=tl),
        grid=(bsz // nb, L // tl),
        in_specs=[tok(D_MODEL), par("norm_mix"), par("w_gates"),
                  tok(SSM_WIDTH), tok(SGU_WIDTH), tok(SB_WIDTH),
                  par("w_branch_a"), par("w_branch_b"), par("w_branch_c"), par("w_out")],
        out_specs=tok(D_MODEL),
        out_shape=jax.ShapeDtypeStruct(h.shape, F32),
        compiler_params=_params(("parallel", "parallel")),
        name="merge",
    )(h, sp["norm_mix"], sp["w_gates"], ya, yb, yc, sp["w_branch_a"], sp["w_branch_b"],
      sp["w_branch_c"], sp["w_out"])


def _ffn_kernel(h_ref, g_ref, wgu_ref, wd_ref, gfin_ref, out_ref, act_scr, *, final):
    x = h_ref[...]
    xn = _rms(x, g_ref[...]).astype(BF16)
    for c0 in range(0, FFN_HIDDEN, FFN_CHUNK):
        gate = _dot(xn, wgu_ref[:, c0:c0 + FFN_CHUNK])
        up = _dot(xn, wgu_ref[:, FFN_HIDDEN + c0:FFN_HIDDEN + c0 + FFN_CHUNK])
        act_scr[:, c0:c0 + FFN_CHUNK] = (gate * _sigmoid(gate) * up).astype(BF16)
    y = x + _dot(act_scr[...], wd_ref[...])
    if final:
        y = _rms(y, gfin_ref[...])
    out_ref[...] = y


def _ffn(h2d, sp, layer, g_final, final, tm):
    T = h2d.shape[0]
    tok = pl.BlockSpec((tm, D_MODEL), lambda i: (i, 0))
    return pl.pallas_call(
        functools.partial(_ffn_kernel, final=final),
        grid=(T // tm,),
        in_specs=[tok, _layer_block(sp["norm_ffn"], layer), _layer_block(sp["w_gate_up"], layer),
                  _layer_block(sp["w_down"], layer), _resident((1, D_MODEL))],
        out_specs=tok,
        out_shape=jax.ShapeDtypeStruct(h2d.shape, F32),
        scratch_shapes=[pltpu.VMEM((tm, FFN_HIDDEN), BF16)],
        compiler_params=_params(("parallel",)),
        name="ffn",
    )(h2d, sp["norm_ffn"], sp["w_gate_up"], sp["w_down"], g_final)


def _cmul(a, b):
    return a[0] * b[0] - a[1] * b[1], a[0] * b[1] + a[1] * b[0]


def _s5_tables(a_re, a_im, b_re, b_im, c_re, c_im, log_dt):
    dt = jnp.exp(log_dt)[:, None]
    mag = jnp.exp(a_re * dt)
    ab = (mag * jnp.cos(a_im * dt), mag * jnp.sin(a_im * dt))
    num_re, num_im = ab[0] - 1.0, ab[1]
    den = a_re * a_re + a_im * a_im
    f_re = (num_re * a_re + num_im * a_im) / den
    f_im = (num_im * a_re - num_re * a_im) / den
    bb_re = f_re[..., None] * b_re - f_im[..., None] * b_im
    bb_im = f_re[..., None] * b_im + f_im[..., None] * b_re
    eye = jnp.eye(SSM_GROUPS, dtype=F32)
    bd_in = lambda m: jnp.einsum("gpc,gh->gchp", m, eye).reshape(SSM_WIDTH, N_STATE)
    ab3 = (ab[0][..., None], ab[1][..., None])
    abb_re, abb_im = _cmul(ab3, (bb_re, bb_im))
    bmat = jnp.concatenate(
        [jnp.concatenate([bd_in(bb_re), bd_in(bb_im)], axis=1),
         jnp.concatenate([bd_in(abb_re), bd_in(abb_im)], axis=1)], axis=0).astype(BF16)
    bd_out = lambda m: jnp.einsum("gcp,gh->gphc", m, eye).reshape(N_STATE, SSM_WIDTH)
    cmat = jnp.concatenate([bd_out(c_re), -bd_out(c_im)], axis=0).astype(BF16)
    a1 = (ab[0].reshape(1, N_STATE), ab[1].reshape(1, N_STATE))
    pows = [a1]
    for _ in range(SUBLANES - 1):
        pows.append(_cmul(pows[-1], a1))
    row = jnp.arange(SUBLANES)[:, None]
    tab = []
    for k in (1, 2):
        p = pows[(1 << k) - 1]
        keep = row >= (1 << k)
        tab += [jnp.where(keep, p[0], 0.0), jnp.where(keep, p[1], 0.0)]
    tab += [jnp.concatenate([p[0] for p in pows], axis=0),
            jnp.concatenate([p[1] for p in pows], axis=0)]
    return bmat, cmat, jnp.stack(tab)


def _stacked_params(norm_mix, w_in, ssm_a_re, ssm_a_im, ssm_b_re, ssm_b_im, ssm_c_re, ssm_c_im,
                    ssm_d, ssm_log_dt, ssm_w_glu, sgu_norm, sgu_w, sgu_b,
                    w_branch_a, w_branch_b, w_branch_c, w_out, norm_ffn, w_gate_up, w_down):
    depth = w_in.shape[0]
    bmat, cmat, tab = jax.vmap(_s5_tables)(ssm_a_re, ssm_a_im, ssm_b_re, ssm_b_im,
                                           ssm_c_re, ssm_c_im, ssm_log_dt)
    w_kv = w_in[:, :, KV_OFFSET:GATE_OFFSET].astype(BF16)
    sgu_bias = jnp.repeat(jnp.swapaxes(sgu_b, 1, 2), SGU_GROUP_DIM, axis=2)
    return {
        "norm_mix": norm_mix.reshape(depth, 1, D_MODEL),
        "w_mix": w_in[:, :, :KV_OFFSET].astype(BF16),
        "w_kv": w_kv,
        "w_kv_t": jnp.swapaxes(w_kv, 1, 2),
        "w_gates": w_in[:, :, GATE_OFFSET:].astype(BF16),
        "bmat": bmat, "cmat": cmat, "tab": tab,
        "d": ssm_d.reshape(depth, 1, SSM_WIDTH),
        "w_glu": ssm_w_glu.astype(BF16),
        "sgu_norm": sgu_norm.reshape(depth, 1, SGU_WIDTH),
        "sgu": lambda n: (sgu_w[:, :, :n, :n], sgu_bias[:, :n]),
        "w_branch_a": w_branch_a.astype(BF16),
        "w_branch_b": w_branch_b.astype(BF16),
        "w_branch_c": w_branch_c.astype(BF16),
        "w_out": w_out.astype(BF16),
        "norm_ffn": norm_ffn.reshape(depth, 1, D_MODEL),
        "w_gate_up": w_gate_up.astype(BF16),
        "w_down": w_down.astype(BF16),
    }


def _layer(h, s0, k_stack, v_stack, k_cache, v_cache, sp, g_final, layer, depth):
    bsz, L, _ = h.shape
    prompt = k_cache is None
    if prompt:
        nb, tl = 1, min(L, TOKEN_TILE)
        tb, n = tl, min(L, SGU_CHUNK)
    else:
        nb, tl = bsz, L
        tb, n = L, L
    uav, q, k_stack, v_stack = _inproj(h, sp, k_stack, v_stack, layer, depth, nb, tl, prompt)
    ya, yb, vn, sfin = _mixab(uav, s0, sp, layer, tb, n)
    if prompt:
        tq = min(L, ATTN_BLOCK)
        yc = _attention(q, k_stack, v_stack, k_stack, v_stack, layer, tq, tq, 0, 1, True)
    else:
        past = k_cache.shape[4]
        tk = min(past, ATTN_BLOCK)
        yc = _attention(q, k_stack, v_stack, k_cache, v_cache, layer, L, tk, past // tk, 0, False)
    h1 = _merge(h, ya, yb, yc, sp, layer, nb, tl)
    T = bsz * L
    h2 = _ffn(h1.reshape(T, D_MODEL), sp, layer, g_final, layer == depth - 1, min(T, TOKEN_TILE))
    return h2.reshape(bsz, L, D_MODEL), sfin, k_stack, v_stack, vn


def kernel(x_prompt, x_sample, state_ssm_re, state_ssm_im, cache_sb_k, cache_sb_v, norm_mix, w_in, ssm_a_re, ssm_a_im, ssm_b_re, ssm_b_im, ssm_c_re, ssm_c_im, ssm_d, ssm_log_dt, ssm_w_glu, sgu_norm, sgu_w, sgu_b, w_branch_a, w_branch_b, w_branch_c, w_out, norm_ffn, w_gate_up, w_down, norm_final):
    weights = (norm_mix, w_in, ssm_a_re, ssm_a_im, ssm_b_re, ssm_b_im, ssm_c_re, ssm_c_im,
               ssm_d, ssm_log_dt, ssm_w_glu, sgu_norm, sgu_w, sgu_b,
               w_branch_a, w_branch_b, w_branch_c, w_out, norm_ffn, w_gate_up, w_down)
    depth = w_in.shape[0]
    bp, bs = x_prompt.shape[0], x_sample.shape[0]
    g_final = norm_final.reshape(1, D_MODEL)
    zero_state = jnp.zeros((bp, 1, 2 * N_STATE), F32)
    hp, hs = x_prompt, x_sample
    kp = vp = ks = vs = None
    cache_k_t = jnp.swapaxes(cache_sb_k, -1, -2)
    cache_v_t = jnp.swapaxes(cache_sb_v, -1, -2)
    sp = _stacked_params(*weights)
    outs = [[] for _ in range(5)]
    for i in range(depth):
        hp, sfin_p, kp, vp, _ = _layer(hp, zero_state, kp, vp, None, None, sp, g_final, i, depth)
        s0 = jnp.concatenate([state_ssm_re[i].reshape(bs, 1, N_STATE),
                              state_ssm_im[i].reshape(bs, 1, N_STATE)], axis=-1)
        hs, sfin_s, ks, vs, vn_s = _layer(hs, s0, ks, vs, cache_k_t, cache_v_t, sp, g_final,
                                          i, depth)
        state = lambda s, part: s[:, 0, part * N_STATE:(part + 1) * N_STATE].reshape(
            -1, SSM_GROUPS, SSM_STATE)
        for lst, val in zip(outs, (state(sfin_p, 0), state(sfin_p, 1),
                                   state(sfin_s, 0), state(sfin_s, 1), vn_s)):
            lst.append(val)
    p_re, p_im, s_re, s_im, s_vb = (jnp.stack(lst) for lst in outs)
    return (hp, hs, p_re, p_im, jnp.swapaxes(kp, -1, -2), jnp.swapaxes(vp, -1, -2),
            s_re, s_im, ks, vs, s_vb)
```

```python
import functools
import math

import jax
import jax.numpy as jnp
from jax import lax
from jax.experimental import pallas as pl
from jax.experimental.pallas import tpu as pltpu

F32 = jnp.float32
BF16 = jnp.bfloat16

D_MODEL = 1024
SSM_WIDTH = 256
SSM_GROUP = 16
SSM_GROUPS = 16
SSM_STATE = 64
N_STATE = SSM_GROUPS * SSM_STATE
SGU_WIDTH = 256
SGU_CHUNK = 128
SGU_GROUPS = 4
SGU_GROUP_DIM = SGU_WIDTH // SGU_GROUPS
SB_HEADS = 8
SB_HEAD_DIM = 64
SB_WIDTH = SB_HEADS * SB_HEAD_DIM
FFN_HIDDEN = 2816
MIX_WIDTH = SSM_WIDTH + 2 * SGU_WIDTH
QKV_OFFSET = MIX_WIDTH
KV_OFFSET = MIX_WIDTH + SB_WIDTH
GATE_OFFSET = MIX_WIDTH + 3 * SB_WIDTH
EPS = 1e-6

SUBLANES = 8
VMEM_LIMIT = 56 * 1024 * 1024
QK_SCALE = SB_HEAD_DIM ** -0.5 * math.log2(math.e)
DEAD_BITS = 100.0 * math.log2(math.e)
MASKED_SCORE = -1e30
ATTN_BLOCK = 256
FFN_CHUNK = 256
TOKEN_TILE = 512


def _rms(x, g):
    return x * lax.rsqrt(jnp.mean(x * x, axis=-1, keepdims=True) + EPS) * g


def _dot(a, b):
    return jnp.dot(a, b, preferred_element_type=F32)


def _sigmoid(x):
    return 1.0 / (1.0 + jnp.exp(-x))


def _split_bf16(x):
    hi = x.astype(BF16)
    lo = (x - hi.astype(F32)).astype(BF16)
    return hi, lo


def _params(semantics):
    return pltpu.CompilerParams(dimension_semantics=semantics, vmem_limit_bytes=VMEM_LIMIT)


def _resident(shape):
    zeros = (0,) * len(shape)
    return pl.BlockSpec(shape, lambda *_: zeros, pipeline_mode=pl.Buffered(1))


def _layer_block(arr, layer):
    zeros = (0,) * (arr.ndim - 1)
    return pl.BlockSpec((None,) + arr.shape[1:], lambda *_: (layer,) + zeros,
                        pipeline_mode=pl.Buffered(1))


def _inproj_kernel(*refs, nb, tl, transposed, fresh):
    if fresh:
        h_ref, g_ref, w_ref, wkv_ref, uav_ref, q_ref, k_ref, v_ref = refs
    else:
        h_ref, g_ref, w_ref, wkv_ref, _, _, uav_ref, q_ref, k_ref, v_ref = refs
    x = h_ref[...].reshape(nb * tl, D_MODEL)
    xn = _rms(x, g_ref[...]).astype(BF16)
    uav_ref[...] = _dot(xn, w_ref[:, 0:MIX_WIDTH]).reshape(nb, tl, MIX_WIDTH)
    q = _dot(xn, w_ref[:, QKV_OFFSET:QKV_OFFSET + SB_WIDTH]) * QK_SCALE
    q_ref[...] = q.astype(BF16).reshape(nb, tl, SB_WIDTH)
    dh = SB_HEAD_DIM
    for idx, ref in enumerate((k_ref, v_ref)):
        if transposed:
            seg = lax.dot_general(wkv_ref[idx * SB_WIDTH:(idx + 1) * SB_WIDTH, :], xn,
                                  (((1,), (1,)), ((), ())), preferred_element_type=F32)
        else:
            seg = _dot(xn, wkv_ref[:, idx * SB_WIDTH:(idx + 1) * SB_WIDTH])
        for bb in range(nb):
            for hh in range(SB_HEADS):
                if transposed:
                    ref[0, bb, hh] = seg[hh * dh:(hh + 1) * dh, bb * tl:(bb + 1) * tl]
                else:
                    ref[0, bb, hh] = seg[bb * tl:(bb + 1) * tl, hh * dh:(hh + 1) * dh]
        if fresh:
            ref[1:] = jnp.zeros((ref.shape[0] - 1,) + ref.shape[1:], F32)


def _inproj(h, sp, k_stack, v_stack, layer, depth, nb, tl, transposed):
    bsz, L, _ = h.shape
    fresh = k_stack is None
    w_kv = sp["w_kv_t"] if transposed else sp["w_kv"]
    minor = (SB_HEAD_DIM, L) if transposed else (L, SB_HEAD_DIM)
    minor_blk = (SB_HEAD_DIM, tl) if transposed else (tl, SB_HEAD_DIM)
    stack_shape = jax.ShapeDtypeStruct((depth, bsz, SB_HEADS) + minor, F32)
    pos = (lambda i: (0, i)) if transposed else (lambda i: (i, 0))
    first = 0 if fresh else layer
    stack_spec = pl.BlockSpec((depth if fresh else 1, nb, SB_HEADS) + minor_blk,
                              lambda b, i: (first, b, 0) + pos(i))
    tok = lambda width: pl.BlockSpec((nb, tl, width), lambda b, i: (b, i, 0))
    in_specs = [tok(D_MODEL), _layer_block(sp["norm_mix"], layer),
                _layer_block(sp["w_mix"], layer), _layer_block(w_kv, layer)]
    args = [h, sp["norm_mix"], sp["w_mix"], w_kv]
    aliases = {}
    if not fresh:
        in_specs += [pl.BlockSpec(memory_space=pl.ANY)] * 2
        args += [k_stack, v_stack]
        aliases = {4: 2, 5: 3}
    return pl.pallas_call(
        functools.partial(_inproj_kernel, nb=nb, tl=tl, transposed=transposed, fresh=fresh),
        grid=(bsz // nb, L // tl),
        in_specs=in_specs,
        out_specs=[tok(MIX_WIDTH), tok(SB_WIDTH), stack_spec, stack_spec],
        out_shape=[jax.ShapeDtypeStruct((bsz, L, MIX_WIDTH), F32),
                   jax.ShapeDtypeStruct((bsz, L, SB_WIDTH), BF16), stack_shape, stack_shape],
        input_output_aliases=aliases,
        compiler_params=_params(("parallel", "parallel")),
        name="inproj",
    )(*args)


def _gelu_tanh(y):
    c = math.sqrt(2.0 / math.pi)
    return 0.5 * y * (1.0 + jnp.tanh(c * (y + 0.044715 * (y * y * y))))


def _shifted_pair(u):
    tile_row = lax.broadcasted_iota(jnp.int32, u.shape, 0) % SUBLANES
    u_before = jnp.where(tile_row == 0, 0.0, pltpu.roll(u, 1, 0))
    return jnp.concatenate([u, u_before], axis=1).astype(BF16)


def _scan_tile(x_scr, tab_ref, j, carry):
    half = N_STATE // 2
    rows = slice(j * SUBLANES, (j + 1) * SUBLANES)
    new_carry = []
    for hf in range(2):
        re_cols = slice(hf * half, (hf + 1) * half)
        im_cols = slice(N_STATE + hf * half, N_STATE + (hf + 1) * half)
        xr = x_scr[rows, re_cols]
        xi = x_scr[rows, im_cols]
        for k in (1, 2):
            sr = pltpu.roll(xr, 1 << k, 0)
            si = pltpu.roll(xi, 1 << k, 0)
            ar = tab_ref[2 * k - 2, :, re_cols]
            ai = tab_ref[2 * k - 1, :, re_cols]
            xr, xi = xr + (ar * sr - ai * si), xi + (ar * si + ai * sr)
        cr, ci = carry[2 * hf], carry[2 * hf + 1]
        pr = tab_ref[4, :, re_cols]
        pi_ = tab_ref[5, :, re_cols]
        xr, xi = xr + (pr * cr - pi_ * ci), xi + (pr * ci + pi_ * cr)
        x_scr[rows, re_cols] = xr
        x_scr[rows, im_cols] = xi
        new_carry += [xr[SUBLANES - 1:SUBLANES], xi[SUBLANES - 1:SUBLANES]]
    return tuple(new_carry)


def _split_state(s):
    half = N_STATE // 2
    return (s[:, 0:half], s[:, N_STATE:N_STATE + half],
            s[:, half:N_STATE], s[:, N_STATE + half:2 * N_STATE])


def _sgu_chunk(ub, vb, lng_ref, w_tril, bias):
    n = vb.shape[0]
    group = lax.broadcasted_iota(jnp.int32, (n, SGU_WIDTH), 1) // SGU_GROUP_DIM
    xc = vb - jnp.mean(vb, axis=-1, keepdims=True)
    vn = xc * lax.rsqrt(jnp.mean(xc * xc, axis=-1, keepdims=True) + EPS) * lng_ref[...]
    vnb = vn.astype(BF16)
    mixed = bias
    for g in range(SGU_GROUPS):
        mixed = mixed + jnp.where(group == g, _dot(w_tril[g], vnb), 0.0)
    return (ub * mixed).astype(BF16), vn


def _tril_weights(wsgu_ref, n):
    row = lax.broadcasted_iota(jnp.int32, (n, n), 0)
    col = lax.broadcasted_iota(jnp.int32, (n, n), 1)
    return [jnp.where(row >= col, wsgu_ref[g], 0.0).astype(BF16) for g in range(SGU_GROUPS)]


def _glu_out(y, wglu_ref):
    z = _dot(_gelu_tanh(y).astype(BF16), wglu_ref[...])
    return (z[:, 0:SSM_WIDTH] * _sigmoid(z[:, SSM_WIDTH:])).astype(BF16)


def _mixab_streams_kernel(uav_ref, s0_ref, bmat_ref, tab_ref, cmat_ref, d_ref, wglu_ref,
                          lng_ref, wsgu_ref, bsgu_ref,
                          ya_ref, yb_ref, vn_ref, sfin_ref, x_scr, *, nb, L):
    m = nb * L
    uav = uav_ref[...].reshape(m, MIX_WIDTH)
    u = uav[:, 0:SSM_WIDTH]
    x_scr[...] = _dot(_shifted_pair(u), bmat_ref[...])
    tiles = L // SUBLANES
    for s in range(nb):
        carry = _split_state(s0_ref[s])
        for j in range(tiles):
            carry = _scan_tile(x_scr, tab_ref, s * tiles + j, carry)
        sfin_ref[s] = jnp.concatenate([carry[0], carry[2], carry[1], carry[3]], axis=1)
    y = _dot(x_scr[...].astype(BF16), cmat_ref[...]) + d_ref[...] * u
    ya_ref[...] = _glu_out(y, wglu_ref).reshape(nb, L, SSM_WIDTH)
    w_tril = _tril_weights(wsgu_ref, L)
    for s in range(nb):
        rows = slice(s * L, (s + 1) * L)
        yb_ref[s], vn_ref[s] = _sgu_chunk(uav[rows, SSM_WIDTH:SSM_WIDTH + SGU_WIDTH],
                                          uav[rows, SSM_WIDTH + SGU_WIDTH:MIX_WIDTH],
                                          lng_ref, w_tril, bsgu_ref[...])


def _mixab_kernel(uav_ref, uavp_ref, s0_ref, bmat_ref, tab_ref, cmat_ref, d_ref, wglu_ref,
                  lng_ref, wsgu_ref, bsgu_ref,
                  ya_ref, yb_ref, vn_ref, sfin_ref,
                  x_scr, carry_scr, *, tb, n):
    i = pl.program_id(1)
    nblk = pl.num_programs(1) - 1
    half = N_STATE // 2

    @pl.when(i == 0)
    def _():
        carry_scr[...] = s0_ref[0]
        x_scr[...] = jnp.zeros((tb, 2 * N_STATE), F32)

    up = uavp_ref[0, :, 0:SSM_WIDTH]
    u_pair = _shifted_pair(uav_ref[0, :, 0:SSM_WIDTH])
    y_parts = []
    for rows in (slice(0, tb // 2), slice(tb // 2, tb)):
        y_parts.append(_dot(x_scr[rows, :].astype(BF16), cmat_ref[...]) + d_ref[...] * up[rows])
        x_scr[rows, :] = _dot(u_pair[rows], bmat_ref[...])
    y = jnp.concatenate(y_parts, axis=0)

    def glu_piece(rows):
        def piece():
            ya_ref[0, rows] = _glu_out(y[rows], wglu_ref)
        return piece

    w_tril = _tril_weights(wsgu_ref, n)

    def sgu_piece(rows):
        def piece():
            yb_ref[0, rows], vn_ref[0, rows] = _sgu_chunk(
                uavp_ref[0, rows, SSM_WIDTH:SSM_WIDTH + SGU_WIDTH],
                uavp_ref[0, rows, SSM_WIDTH + SGU_WIDTH:MIX_WIDTH], lng_ref, w_tril, bsgu_ref[...])
        return piece

    glu_rows = min(tb, 2 * SGU_CHUNK)
    pieces = []
    for r0 in range(0, tb, glu_rows):
        pieces.append(glu_piece(slice(r0, r0 + glu_rows)))
        pieces += [sgu_piece(slice(c0, c0 + n)) for c0 in range(r0, r0 + glu_rows, n)]

    n_tiles = tb // SUBLANES
    every = max(n_tiles // len(pieces), 1)
    carry_in = _split_state(carry_scr[...])
    c = carry_in
    for j in range(n_tiles):
        c = _scan_tile(x_scr, tab_ref, j, c)
        if pieces and (j + 1) % every == 0:
            pieces.pop(0)()
    for piece in pieces:
        piece()

    keep = i < nblk
    for part, cols in enumerate((slice(0, half), slice(N_STATE, N_STATE + half),
                                 slice(half, N_STATE), slice(N_STATE + half, 2 * N_STATE))):
        carry_scr[:, cols] = jnp.where(keep, c[part], carry_in[part])
    sfin_ref[0] = carry_scr[...]


def _mixab(uav, s0, sp, layer, tb, n):
    bsz, L, _ = uav.shape
    sgu_w, sgu_bias = sp["sgu"](n)
    params = [sp["bmat"], sp["tab"], sp["cmat"], sp["d"], sp["w_glu"], sp["sgu_norm"],
              sgu_w, sgu_bias]
    nblk = L // tb
    cur = lambda b, i: (b, jnp.minimum(i, nblk - 1), 0)
    prev = lambda b, i: (b, jnp.maximum(i - 1, 0), 0)
    tok = lambda width, dtype: (pl.BlockSpec((1, tb, width), prev),
                                jax.ShapeDtypeStruct((bsz, L, width), dtype))
    ya_spec, ya_shape = tok(SSM_WIDTH, BF16)
    yb_spec, yb_shape = tok(SGU_WIDTH, BF16)
    vn_spec, vn_shape = tok(SGU_WIDTH, F32)
    state_spec = pl.BlockSpec((1, 1, 2 * N_STATE), lambda b, i: (b, 0, 0))
    return pl.pallas_call(
        functools.partial(_mixab_kernel, tb=tb, n=n),
        grid=(bsz, nblk + 1),
        in_specs=[pl.BlockSpec((1, tb, MIX_WIDTH), cur),
                  pl.BlockSpec((1, tb, MIX_WIDTH), prev),
                  state_spec] + [_layer_block(p, layer) for p in params],
        out_specs=[ya_spec, yb_spec, vn_spec, state_spec],
        out_shape=[ya_shape, yb_shape, vn_shape,
                   jax.ShapeDtypeStruct((bsz, 1, 2 * N_STATE), F32)],
        scratch_shapes=[pltpu.VMEM((tb, 2 * N_STATE), F32), pltpu.VMEM((1, 2 * N_STATE), F32)],
        compiler_params=_params(("parallel", "arbitrary")),
        name="mixab",
    )(uav, uav, s0, *params)


def _mixab_streams(uav, s0, sp, layer):
    bsz, L, _ = uav.shape
    sgu_w, sgu_bias = sp["sgu"](L)
    params = [sp["bmat"], sp["tab"], sp["cmat"], sp["d"], sp["w_glu"], sp["sgu_norm"],
              sgu_w, sgu_bias]
    tok = lambda width, dtype: (pl.BlockSpec((bsz, L, width), lambda i: (0, 0, 0)),
                                jax.ShapeDtypeStruct((bsz, L, width), dtype))
    ya_spec, ya_shape = tok(SSM_WIDTH, BF16)
    yb_spec, yb_shape = tok(SGU_WIDTH, BF16)
    vn_spec, vn_shape = tok(SGU_WIDTH, F32)
    state_spec = pl.BlockSpec((bsz, 1, 2 * N_STATE), lambda i: (0, 0, 0))
    return pl.pallas_call(
        functools.partial(_mixab_streams_kernel, nb=bsz, L=L),
        grid=(1,),
        in_specs=[tok(MIX_WIDTH, F32)[0], state_spec] + [_layer_block(p, layer) for p in params],
        out_specs=[ya_spec, yb_spec, vn_spec, state_spec],
        out_shape=[ya_shape, yb_shape, vn_shape,
                   jax.ShapeDtypeStruct((bsz, 1, 2 * N_STATE), F32)],
        scratch_shapes=[pltpu.VMEM((bsz * L, 2 * N_STATE), F32)],
        compiler_params=_params(("arbitrary",)),
        name="mixab_streams",
    )(uav, s0, *params)


def _attn_kernel(q_ref, kd_ref, vd_ref, kp_ref, vp_ref, kh_hbm, vh_hbm, o_ref, kbuf, vbuf, sem, *,
                 layer, tq, tk, hist0, hist_per_q, diag_transposed):
    b = pl.program_id(0)
    i = pl.program_id(1)
    n_hist = hist0 + i * hist_per_q
    heads = range(SB_HEADS)
    dh = SB_HEAD_DIM
    nt = (((1,), (1,)), ((), ()))
    qs = [q_ref[0, :, h * dh:(h + 1) * dh] for h in heads]

    def suffix_ones(size):
        r = lax.broadcasted_iota(jnp.int32, (size, size), 0)
        c = lax.broadcasted_iota(jnp.int32, (size, size), 1)
        return jnp.where(r >= c, 1.0, 0.0).astype(BF16)

    def decay(zs, ones):
        parts = []
        for z2 in zs:
            sp = jnp.maximum(z2, jnp.log2(1.0 + jnp.exp2(jnp.minimum(z2, 126.0))))
            parts += _split_bf16(sp)
        sums = _dot(jnp.concatenate(parts, axis=0), ones)
        return [sums[(2 * h) * tq:(2 * h + 1) * tq] + sums[(2 * h + 1) * tq:(2 * h + 2) * tq]
                for h in range(len(zs))]

    def weights(z2, dec, dead):
        return jnp.exp2(z2 - (dec if dead is None else dec + dead)).astype(BF16)

    def history(k_t, v_t, ones, deads, accs, scale):
        zs = [_dot(qs[h], k_t[h]) for h in heads]
        decs = decay(zs, ones)
        new_deads, new_accs = [], []
        for h in heads:
            pv = lax.dot_general(weights(zs[h], decs[h], deads[h]), v_t[h], nt,
                                 preferred_element_type=F32)
            total = decs[h][:, 0:1]
            if scale is not None:
                pv, total = scale * pv, scale * total
            new_accs.append(accs[h] + pv)
            new_deads.append(deads[h] + total)
        return tuple(new_deads), tuple(new_accs)

    r = lax.broadcasted_iota(jnp.int32, (tq, tq), 0)
    c = lax.broadcasted_iota(jnp.int32, (tq, tq), 1)
    ones_q = suffix_ones(tq)
    ones_k = ones_q if tk == tq else suffix_ones(tk)
    if diag_transposed:
        zd = [_dot(qs[h], kd_ref[0, 0, h].astype(BF16)) for h in heads]
    else:
        zd = [lax.dot_general(qs[h], kd_ref[0, 0, h].astype(BF16), nt, preferred_element_type=F32)
              for h in heads]
    zd = [jnp.where(c < r, z, MASKED_SCORE) for z in zd]
    dd = decay(zd, ones_q)
    accs = []
    for h in heads:
        w = weights(zd[h], dd[h], None)
        vd = vd_ref[0, 0, h].astype(BF16)
        accs.append(lax.dot_general(w, vd, nt, preferred_element_type=F32) if diag_transposed
                    else _dot(w, vd))
    deads = tuple(d[:, 0:1] for d in dd)

    has_prev = (n_hist > 0).astype(F32)
    deads, accs = history([kp_ref[0, 0, h].astype(BF16) for h in heads],
                          [vp_ref[0, 0, h].astype(BF16) for h in heads],
                          ones_k, deads, tuple(accs), has_prev)

    def fetch(j):
        cols = pl.ds(pl.multiple_of(j * tk, tk), tk)
        return (pltpu.make_async_copy(kh_hbm.at[layer, b, :, :, cols], kbuf, sem.at[0]),
                pltpu.make_async_copy(vh_hbm.at[layer, b, :, :, cols], vbuf, sem.at[1]))

    def more(carry):
        j, deads, _ = carry
        least = deads[0]
        for h in heads[1:]:
            least = jnp.minimum(least, deads[h])
        return jnp.logical_and(j >= 0, jnp.min(least) < DEAD_BITS)

    def step(carry):
        j, deads, accs = carry
        copies = fetch(j)
        for cp in copies:
            cp.start()
        for cp in copies:
            cp.wait()
        deads, accs = history([kbuf[h].astype(BF16) for h in heads],
                              [vbuf[h].astype(BF16) for h in heads], ones_k, deads, accs, None)
        return j - 1, deads, accs

    _, _, accs = lax.while_loop(more, step, (n_hist - 2, deads, accs))
    o_ref[0] = jnp.concatenate(accs, axis=1).astype(o_ref.dtype)


def _attention(q, k_new, v_new, k_hist, v_hist, layer, tq, tk, hist0, hist_per_q, diag_transposed):
    bsz, L, _ = q.shape
    H, Dh = SB_HEADS, SB_HEAD_DIM
    if diag_transposed:
        diag_spec = pl.BlockSpec((1, 1, H, Dh, tq), lambda b, i: (layer, b, 0, 0, i))
    else:
        diag_spec = pl.BlockSpec((1, 1, H, tq, Dh), lambda b, i: (layer, b, 0, i, 0))
    prev_spec = pl.BlockSpec(
        (1, 1, H, Dh, tk),
        lambda b, i: (layer, b, 0, 0, jnp.maximum(hist0 + i * hist_per_q - 1, 0)))
    any_spec = pl.BlockSpec(memory_space=pl.ANY)
    tok = pl.BlockSpec((1, tq, H * Dh), lambda b, i: (b, i, 0))
    return pl.pallas_call(
        functools.partial(_attn_kernel, layer=layer, tq=tq, tk=tk, hist0=hist0,
                          hist_per_q=hist_per_q, diag_transposed=diag_transposed),
        grid=(bsz, L // tq),
        in_specs=[tok, diag_spec, diag_spec, prev_spec, prev_spec, any_spec, any_spec],
        out_specs=tok,
        out_shape=jax.ShapeDtypeStruct((bsz, L, H * Dh), BF16),
        scratch_shapes=[pltpu.VMEM((H, Dh, tk), F32), pltpu.VMEM((H, Dh, tk), F32),
                        pltpu.SemaphoreType.DMA((2,))],
        compiler_params=_params(("parallel", "arbitrary")),
        name="stickbreak",
    )(q, k_new, v_new, k_hist, v_hist, k_hist, v_hist)


def _merge_kernel(h_ref, g_ref, wg_ref, ya_ref, yb_ref, yc_ref, wa_ref, wb_ref, wc_ref, wout_ref,
                  hout_ref, *, nb, tl):
    m = nb * tl
    x = h_ref[...].reshape(m, D_MODEL)
    xn = _rms(x, g_ref[...]).astype(BF16)

    def gate(idx):
        return _sigmoid(_dot(xn, wg_ref[:, idx * D_MODEL:(idx + 1) * D_MODEL]))

    merged = gate(0) * _dot(ya_ref[...].reshape(m, SSM_WIDTH), wa_ref[...])
    merged = merged + gate(1) * _dot(yb_ref[...].reshape(m, SGU_WIDTH), wb_ref[...])
    merged = merged + gate(2) * _dot(yc_ref[...].reshape(m, SB_WIDTH), wc_ref[...])
    hout_ref[...] = (x + _dot(merged.astype(BF16), wout_ref[...])).reshape(nb, tl, D_MODEL)


def _merge(h, ya, yb, yc, sp, layer, nb, tl):
    bsz, L, _ = h.shape
    tok = lambda width: pl.BlockSpec((nb, tl, width), lambda b, i: (b, i, 0))
    par = lambda name: _layer_block(sp[name], layer)
    return pl.pallas_call(
        functools.partial(_merge_kernel, nb=nb, tl=tl),
        grid=(bsz // nb, L // tl),
        in_specs=[tok(D_MODEL), par("norm_mix"), par("w_gates"),
                  tok(SSM_WIDTH), tok(SGU_WIDTH), tok(SB_WIDTH),
                  par("w_branch_a"), par("w_branch_b"), par("w_branch_c"), par("w_out")],
        out_specs=tok(D_MODEL),
        out_shape=jax.ShapeDtypeStruct(h.shape, F32),
        compiler_params=_params(("parallel", "parallel")),
        name="merge",
    )(h, sp["norm_mix"], sp["w_gates"], ya, yb, yc, sp["w_branch_a"], sp["w_branch_b"],
      sp["w_branch_c"], sp["w_out"])


def _ffn_kernel(h_ref, g_ref, wgu_ref, wd_ref, gfin_ref, out_ref, act_scr, *, final):
    x = h_ref[...]
    xn = _rms(x, g_ref[...]).astype(BF16)
    for c0 in range(0, FFN_HIDDEN, FFN_CHUNK):
        gate = _dot(xn, wgu_ref[:, c0:c0 + FFN_CHUNK])
        up = _dot(xn, wgu_ref[:, FFN_HIDDEN + c0:FFN_HIDDEN + c0 + FFN_CHUNK])
        act_scr[:, c0:c0 + FFN_CHUNK] = (gate * _sigmoid(gate) * up).astype(BF16)
    y = x + _dot(act_scr[...], wd_ref[...])
    if final:
        y = _rms(y, gfin_ref[...])
    out_ref[...] = y


def _ffn(h2d, sp, layer, g_final, final, tm):
    T = h2d.shape[0]
    tok = pl.BlockSpec((tm, D_MODEL), lambda i: (i, 0))
    return pl.pallas_call(
        functools.partial(_ffn_kernel, final=final),
        grid=(T // tm,),
        in_specs=[tok, _layer_block(sp["norm_ffn"], layer), _layer_block(sp["w_gate_up"], layer),
                  _layer_block(sp["w_down"], layer), _resident((1, D_MODEL))],
        out_specs=tok,
        out_shape=jax.ShapeDtypeStruct(h2d.shape, F32),
        scratch_shapes=[pltpu.VMEM((tm, FFN_HIDDEN), BF16)],
        compiler_params=_params(("parallel",)),
        name="ffn",
    )(h2d, sp["norm_ffn"], sp["w_gate_up"], sp["w_down"], g_final)


def _cmul(a, b):
    return a[0] * b[0] - a[1] * b[1], a[0] * b[1] + a[1] * b[0]


def _s5_tables(a_re, a_im, b_re, b_im, c_re, c_im, log_dt):
    dt = jnp.exp(log_dt)[:, None]
    mag = jnp.exp(a_re * dt)
    ab = (mag * jnp.cos(a_im * dt), mag * jnp.sin(a_im * dt))
    num_re, num_im = ab[0] - 1.0, ab[1]
    den = a_re * a_re + a_im * a_im
    f_re = (num_re * a_re + num_im * a_im) / den
    f_im = (num_im * a_re - num_re * a_im) / den
    bb_re = f_re[..., None] * b_re - f_im[..., None] * b_im
    bb_im = f_re[..., None] * b_im + f_im[..., None] * b_re
    eye = jnp.eye(SSM_GROUPS, dtype=F32)
    bd_in = lambda m: jnp.einsum("gpc,gh->gchp", m, eye).reshape(SSM_WIDTH, N_STATE)
    ab3 = (ab[0][..., None], ab[1][..., None])
    abb_re, abb_im = _cmul(ab3, (bb_re, bb_im))
    bmat = jnp.concatenate(
        [jnp.concatenate([bd_in(bb_re), bd_in(bb_im)], axis=1),
         jnp.concatenate([bd_in(abb_re), bd_in(abb_im)], axis=1)], axis=0).astype(BF16)
    bd_out = lambda m: jnp.einsum("gcp,gh->gphc", m, eye).reshape(N_STATE, SSM_WIDTH)
    cmat = jnp.concatenate([bd_out(c_re), -bd_out(c_im)], axis=0).astype(BF16)
    a1 = (ab[0].reshape(1, N_STATE), ab[1].reshape(1, N_STATE))
    pows = [a1]
    for _ in range(SUBLANES - 1):
        pows.append(_cmul(pows[-1], a1))
    row = jnp.arange(SUBLANES)[:, None]
    tab = []
    for k in (1, 2):
        p = pows[(1 << k) - 1]
        keep = row >= (1 << k)
        tab += [jnp.where(keep, p[0], 0.0), jnp.where(keep, p[1], 0.0)]
    tab += [jnp.concatenate([p[0] for p in pows], axis=0),
            jnp.concatenate([p[1] for p in pows], axis=0)]
    return bmat, cmat, jnp.stack(tab)


def _stacked_params(norm_mix, w_in, ssm_a_re, ssm_a_im, ssm_b_re, ssm_b_im, ssm_c_re, ssm_c_im,
                    ssm_d, ssm_log_dt, ssm_w_glu, sgu_norm, sgu_w, sgu_b,
                    w_branch_a, w_branch_b, w_branch_c, w_out, norm_ffn, w_gate_up, w_down):
    depth = w_in.shape[0]
    bmat, cmat, tab = jax.vmap(_s5_tables)(ssm_a_re, ssm_a_im, ssm_b_re, ssm_b_im,
                                           ssm_c_re, ssm_c_im, ssm_log_dt)
    w_kv = w_in[:, :, KV_OFFSET:GATE_OFFSET].astype(BF16)
    sgu_bias = jnp.repeat(jnp.swapaxes(sgu_b, 1, 2), SGU_GROUP_DIM, axis=2)
    return {
        "norm_mix": norm_mix.reshape(depth, 1, D_MODEL),
        "w_mix": w_in[:, :, :KV_OFFSET].astype(BF16),
        "w_kv": w_kv,
        "w_kv_t": jnp.swapaxes(w_kv, 1, 2),
        "w_gates": w_in[:, :, GATE_OFFSET:].astype(BF16),
        "bmat": bmat, "cmat": cmat, "tab": tab,
        "d": ssm_d.reshape(depth, 1, SSM_WIDTH),
        "w_glu": ssm_w_glu.astype(BF16),
        "sgu_norm": sgu_norm.reshape(depth, 1, SGU_WIDTH),
        "sgu": lambda n: (sgu_w[:, :, :n, :n], sgu_bias[:, :n]),
        "w_branch_a": w_branch_a.astype(BF16),
        "w_branch_b": w_branch_b.astype(BF16),
        "w_branch_c": w_branch_c.astype(BF16),
        "w_out": w_out.astype(BF16),
        "norm_ffn": norm_ffn.reshape(depth, 1, D_MODEL),
        "w_gate_up": w_gate_up.astype(BF16),
        "w_down": w_down.astype(BF16),
    }


def _layer(h, s0, k_stack, v_stack, k_cache, v_cache, sp, g_final, layer, depth):
    bsz, L, _ = h.shape
    prompt = k_cache is None
    if prompt:
        nb, tl = 1, min(L, TOKEN_TILE)
        tb, n = tl, min(L, SGU_CHUNK)
    else:
        nb, tl = bsz, L
        tb = n = None
    uav, q, k_stack, v_stack = _inproj(h, sp, k_stack, v_stack, layer, depth, nb, tl, prompt)
    ya, yb, vn, sfin = (_mixab(uav, s0, sp, layer, tb, n) if prompt
                        else _mixab_streams(uav, s0, sp, layer))
    if prompt:
        tq = min(L, ATTN_BLOCK)
        yc = _attention(q, k_stack, v_stack, k_stack, v_stack, layer, tq, tq, 0, 1, True)
    else:
        past = k_cache.shape[4]
        tk = min(past, ATTN_BLOCK)
        yc = _attention(q, k_stack, v_stack, k_cache, v_cache, layer, L, tk, past // tk, 0, False)
    h1 = _merge(h, ya, yb, yc, sp, layer, nb, tl)
    T = bsz * L
    h2 = _ffn(h1.reshape(T, D_MODEL), sp, layer, g_final, layer == depth - 1, min(T, TOKEN_TILE))
    return h2.reshape(bsz, L, D_MODEL), sfin, k_stack, v_stack, vn


def kernel(x_prompt, x_sample, state_ssm_re, state_ssm_im, cache_sb_k, cache_sb_v, norm_mix, w_in, ssm_a_re, ssm_a_im, ssm_b_re, ssm_b_im, ssm_c_re, ssm_c_im, ssm_d, ssm_log_dt, ssm_w_glu, sgu_norm, sgu_w, sgu_b, w_branch_a, w_branch_b, w_branch_c, w_out, norm_ffn, w_gate_up, w_down, norm_final):
    weights = (norm_mix, w_in, ssm_a_re, ssm_a_im, ssm_b_re, ssm_b_im, ssm_c_re, ssm_c_im,
               ssm_d, ssm_log_dt, ssm_w_glu, sgu_norm, sgu_w, sgu_b,
               w_branch_a, w_branch_b, w_branch_c, w_out, norm_ffn, w_gate_up, w_down)
    depth = w_in.shape[0]
    bp, bs = x_prompt.shape[0], x_sample.shape[0]
    g_final = norm_final.reshape(1, D_MODEL)
    zero_state = jnp.zeros((bp, 1, 2 * N_STATE), F32)
    hp, hs = x_prompt, x_sample
    kp = vp = ks = vs = None
    cache_k_t = jnp.swapaxes(cache_sb_k, -1, -2)
    cache_v_t = jnp.swapaxes(cache_sb_v, -1, -2)
    sp = _stacked_params(*weights)
    outs = [[] for _ in range(5)]
    for i in range(depth):
        hp, sfin_p, kp, vp, _ = _layer(hp, zero_state, kp, vp, None, None, sp, g_final, i, depth)
        s0 = jnp.concatenate([state_ssm_re[i].reshape(bs, 1, N_STATE),
                              state_ssm_im[i].reshape(bs, 1, N_STATE)], axis=-1)
        hs, sfin_s, ks, vs, vn_s = _layer(hs, s0, ks, vs, cache_k_t, cache_v_t, sp, g_final,
                                          i, depth)
        state = lambda s, part: s[:, 0, part * N_STATE:(part + 1) * N_STATE].reshape(
            -1, SSM_GROUPS, SSM_STATE)
        for lst, val in zip(outs, (state(sfin_p, 0), state(sfin_p, 1),
                                   state(sfin_s, 0), state(sfin_s, 1), vn_s)):
            lst.append(val)
    p_re, p_im, s_re, s_im, s_vb = (jnp.stack(lst) for lst in outs)
    return (hp, hs, p_re, p_im, jnp.swapaxes(kp, -1, -2), jnp.swapaxes(vp, -1, -2),
            s_re, s_im, ks, vs, s_vb)
```
